```python
import math, functools
import jax, jax.numpy as jnp
from jax import lax
import numpy as np

D_MODEL = 1024
BATCH = 8
SEQ = 2048
DEPTH = 4
DEC_BATCH = 32
DEC_SEQ = 1
PAST_LEN = 8192
PAGE_SIZE = 128

N_HEADS = 8
N_KV_HEADS = 4
HEAD_GROUP = N_HEADS // N_KV_HEADS
HEAD_DIM = D_MODEL // (2 * N_HEADS)
Q_WIDTH = N_HEADS * 2 * HEAD_DIM
K_WIDTH = N_KV_HEADS * 2 * HEAD_DIM
V_WIDTH = N_KV_HEADS * 2 * HEAD_DIM
ATT_WIDTH = N_HEADS * 2 * HEAD_DIM
LRU_WIDTH = D_MODEL
LRU_BLOCKS = 8
LRU_BLOCK_W = LRU_WIDTH // LRU_BLOCKS
LRU_CONV_W = 4
LRU_C = 8.0
SC_WIDTH = D_MODEL
SC_CONV_W = 3
N_BRANCHES = 3
D_FF = ((8 * D_MODEL // 3 + 127) // 128) * 128
IN_WIDTH = Q_WIDTH + K_WIDTH + V_WIDTH + 2 * LRU_WIDTH + 3 * SC_WIDTH + N_BRANCHES * D_MODEL
Q_BLOCK = 128
NORM_EPS = 1e-6

kernel_name = 'hybrid_diffattn_rglru_shortconv_step'


def _rmsnorm(x, g):
    xf = x.astype(jnp.float32)
    y = xf * lax.rsqrt(jnp.mean(xf * xf, axis=-1, keepdims=True) + NORM_EPS)
    return (y * g.astype(jnp.float32)).astype(x.dtype)


def _swiglu(x, w_up, w_down):
    gate, up = jnp.split(x @ w_up, 2, axis=-1)
    return (jax.nn.silu(gate) * up) @ w_down


def _causal_dwconv(x, buf, w):
    k_w, t = w.shape[0], x.shape[1]
    xp = jnp.concatenate([buf.astype(x.dtype), x], axis=1)
    y = xp[:, 0:t] * w[0]
    for j in range(1, k_w):
        y = y + xp[:, j:j + t] * w[j]
    return y, xp[:, t:]


def _alibi_slopes():
    m = 2.0 ** (-8.0 * jnp.arange(1, N_HEADS + 1, dtype=jnp.float32) / N_HEADS)
    return m.reshape(N_KV_HEADS, HEAD_GROUP)


def _diff_attn_prompt(q, k, v, lam):
    b, s = q.shape[0], q.shape[1]
    nb = s // Q_BLOCK
    slopes = _alibi_slopes()
    scale = HEAD_DIM ** -0.5
    qb = jnp.moveaxis(q.reshape(b, nb, Q_BLOCK, N_KV_HEADS, HEAD_GROUP, 2, HEAD_DIM), 1, 0)
    kpos = jnp.arange(s)
    vf = v.astype(jnp.float32)

    def block(args):
        q_blk, n = args
        qpos = n * Q_BLOCK + jnp.arange(Q_BLOCK)
        dist = (qpos[:, None] - kpos[None, :]).astype(jnp.float32)
        sc = jnp.einsum('bqkgmd,bskmd->mbkgqs', q_blk, k, preferred_element_type=jnp.float32) * scale
        sc = sc - slopes[:, :, None, None] * dist
        sc = jnp.where(dist >= 0, sc, -jnp.inf)
        p = jax.nn.softmax(sc, axis=-1)
        w = p[0] - lam * p[1]
        return jnp.einsum('bkgqs,bskd->bqkgd', w, vf)

    o = lax.map(block, (qb, jnp.arange(nb)))
    return jnp.moveaxis(o, 0, 1).reshape(b, s, N_KV_HEADS, HEAD_GROUP, 2 * HEAD_DIM)


def _diff_attn_paged(q, k, v, lam, cache_k, cache_v, layer, page_table):
    t = q.shape[1]
    page = cache_k.shape[2]
    n_pages = page_table.shape[1]
    past = n_pages * page
    slopes = _alibi_slopes()
    scale = HEAD_DIM ** -0.5
    qpos = past + jnp.arange(t)
    dist = (jnp.arange(t)[:, None] - jnp.arange(t)[None, :]).astype(jnp.float32)
    sc = jnp.einsum('bqkgmd,bskmd->mbkgqs', q, k, preferred_element_type=jnp.float32) * scale
    sc = jnp.where(dist >= 0, sc - slopes[:, :, None, None] * dist, -jnp.inf)
    m = jnp.max(sc, axis=-1)
    p = jnp.exp(sc - m[..., None])
    l = jnp.sum(p, axis=-1)
    acc = jnp.einsum('mbkgqs,bskd->mbkgqd', p, v.astype(jnp.float32))

    def page_step(carry, xs):
        m, l, acc = carry
        n, phys = xs
        kp = cache_k[layer, phys]
        vp = cache_v[layer, phys].astype(jnp.float32)
        kpos = n * page + jnp.arange(page)
        d = (qpos[:, None] - kpos[None, :]).astype(jnp.float32)
        s_ = jnp.einsum('bqkgmd,bskmd->mbkgqs', q, kp, preferred_element_type=jnp.float32) * scale
        s_ = s_ - slopes[:, :, None, None] * d
        m_new = jnp.maximum(m, jnp.max(s_, axis=-1))
        corr = jnp.exp(m - m_new)
        p_ = jnp.exp(s_ - m_new[..., None])
        l_new = l * corr + jnp.sum(p_, axis=-1)
        acc_new = acc * corr[..., None] + jnp.einsum('mbkgqs,bskd->mbkgqd', p_, vp)
        return (m_new, l_new, acc_new), None

    (m, l, acc), _ = lax.scan(page_step, (m, l, acc), (jnp.arange(n_pages), page_table.T))
    o = acc[0] / l[0][..., None] - lam * (acc[1] / l[1][..., None])
    return jnp.transpose(o, (0, 3, 1, 2, 4))


def _rglru(x, gate_x, h0, conv_buf, conv_w, conv_b, gate_w, gate_b, lam):
    b, t = x.shape[0], x.shape[1]
    xc, new_buf = _causal_dwconv(x, conv_buf, conv_w)
    xc = (xc + conv_b).astype(jnp.float32)
    g = jnp.einsum('btnc,gncd->gbtnd', xc.reshape(b, t, LRU_BLOCKS, LRU_BLOCK_W), gate_w.astype(jnp.float32))
    g = g.reshape(2, b, t, LRU_WIDTH) + gate_b.astype(jnp.float32)[:, None, None, :]
    r = jax.nn.sigmoid(g[0])
    i = jax.nn.sigmoid(g[1])
    log_a = -LRU_C * r * jax.nn.softplus(-lam.astype(jnp.float32))
    a = jnp.exp(log_a)
    u = jnp.sqrt(-jnp.expm1(2.0 * log_a)) * (i * xc)
    u = u.at[:, 0].add(a[:, 0] * h0.astype(jnp.float32))

    def combine(lhs, rhs):
        return lhs[0] * rhs[0], rhs[0] * lhs[1] + rhs[1]

    _, h = lax.associative_scan(combine, (a, u), axis=1)
    y = h * jax.nn.gelu(gate_x.astype(jnp.float32))
    return y.astype(x.dtype), h[:, -1].astype(h0.dtype), new_buf


def _short_conv(bg, cg, xin, buf, w):
    y, new_buf = _causal_dwconv(cg * xin, buf, w)
    return bg * y, new_buf


def _trunk_layer(x, l, attn_core, rg_h0, rg_buf, sc_buf, p):
    b, t = x.shape[0], x.shape[1]
    g = p['norm_g'][l]
    x = x + 0.5 * _rmsnorm(_swiglu(_rmsnorm(x, g[0]), p['w_ffn_up'][l, 0], p['w_ffn_down'][l, 0]), g[1])
    h = _rmsnorm(x, g[2])
    z = h @ p['w_in'][l]
    cuts = np.cumsum([Q_WIDTH, K_WIDTH, V_WIDTH, LRU_WIDTH, LRU_WIDTH, SC_WIDTH, SC_WIDTH, SC_WIDTH]).tolist()
    q, k, v, rg_x, rg_g, sc_b, sc_c, sc_x, gz = jnp.split(z, cuts, axis=-1)
    q = q.reshape(b, t, N_KV_HEADS, HEAD_GROUP, 2, HEAD_DIM)
    k = k.reshape(b, t, N_KV_HEADS, 2, HEAD_DIM)
    v = v.reshape(b, t, N_KV_HEADS, 2 * HEAD_DIM)
    lam_init = 0.8 - 0.6 * math.exp(-0.3 * l)
    lqk = p['lambda_qk'][l].astype(jnp.float32)
    lam = jnp.exp(jnp.sum(lqk[0] * lqk[1])) - jnp.exp(jnp.sum(lqk[2] * lqk[3])) + lam_init
    o = attn_core(q, k, v, lam)
    o = (_rmsnorm(o, p['subln_g'][l]) * (1.0 - lam_init)).reshape(b, t, ATT_WIDTH).astype(x.dtype)
    y_rg, rg_h, rg_buf_new = _rglru(rg_x, rg_g, rg_h0, rg_buf, p['rg_conv_w'][l], p['rg_conv_b'][l],
                                    p['rg_gate_w'][l], p['rg_gate_b'][l], p['rg_lambda'][l])
    y_sc, sc_buf_new = _short_conv(sc_b, sc_c, sc_x, sc_buf, p['sc_conv_w'][l])
    gates = jax.nn.sigmoid(gz.astype(jnp.float32)).reshape(b, t, N_BRANCHES, D_MODEL)
    merged = (gates[:, :, 0] * (o @ p['w_branch_attn'][l]) + gates[:, :, 1] * (y_rg @ p['w_branch_rg'][l])
              + gates[:, :, 2] * (y_sc @ p['w_branch_sc'][l])).astype(x.dtype)
    x = x + _rmsnorm(merged @ p['w_out'][l], g[3])
    x = x + 0.5 * _rmsnorm(_swiglu(_rmsnorm(x, g[4]), p['w_ffn_up'][l, 1], p['w_ffn_down'][l, 1]), g[5])
    return x, (k, v, rg_h, rg_buf_new, sc_buf_new)


def setup_inputs(seed: int = 0) -> dict:
    key = jax.random.key(seed)
    ks = jax.random.split(key, 24)
    f32 = jnp.float32
    n_pages = PAST_LEN // PAGE_SIZE
    n_used = DEC_BATCH * n_pages
    n_pool = n_used + (n_used + 3) // 4

    def nrm(k, shape, scale=1.0):
        return jax.random.normal(k, shape, f32) * scale

    x_prompt = nrm(ks[0], (BATCH, SEQ, D_MODEL))
    x_sample = nrm(ks[1], (DEC_BATCH, DEC_SEQ, D_MODEL))
    cache_k = nrm(ks[2], (DEPTH, n_pool, PAGE_SIZE, N_KV_HEADS, 2, HEAD_DIM))
    cache_v = nrm(ks[3], (DEPTH, n_pool, PAGE_SIZE, N_KV_HEADS, 2 * HEAD_DIM))
    page_table = jax.random.permutation(ks[4], n_pool)[:n_used].reshape(DEC_BATCH, n_pages).astype(jnp.int32)
    state_rglru_h = nrm(ks[5], (DEPTH, DEC_BATCH, LRU_WIDTH), 0.5)
    state_rglru_conv = nrm(ks[6], (DEPTH, DEC_BATCH, LRU_CONV_W - 1, LRU_WIDTH))
    state_sconv = nrm(ks[7], (DEPTH, DEC_BATCH, SC_CONV_W - 1, SC_WIDTH))
    norm_g = 1.0 + nrm(ks[8], (DEPTH, 6, D_MODEL), 0.02)
    w_ffn_up = nrm(ks[9], (DEPTH, 2, D_MODEL, 2 * D_FF), D_MODEL ** -0.5)
    w_ffn_down = nrm(ks[10], (DEPTH, 2, D_FF, D_MODEL), D_FF ** -0.5)
    w_in = nrm(ks[11], (DEPTH, D_MODEL, IN_WIDTH), D_MODEL ** -0.5)
    lambda_qk = nrm(ks[12], (DEPTH, 4, HEAD_DIM), 0.1)
    subln_g = 1.0 + nrm(ks[13], (DEPTH, 2 * HEAD_DIM), 0.02)
    rg_conv_w = nrm(ks[14], (DEPTH, LRU_CONV_W, LRU_WIDTH), LRU_CONV_W ** -0.5)
    rg_conv_b = nrm(ks[15], (DEPTH, LRU_WIDTH), 0.02)
    rg_gate_w = nrm(ks[16], (DEPTH, 2, LRU_BLOCKS, LRU_BLOCK_W, LRU_BLOCK_W), LRU_BLOCK_W ** -0.5)
    rg_gate_b = nrm(ks[17], (DEPTH, 2, LRU_WIDTH), 0.02)
    a0 = jax.random.uniform(ks[18], (DEPTH, LRU_WIDTH), f32, 0.9, 0.999)
    s0 = a0 ** (1.0 / LRU_C)
    rg_lambda = jnp.log(s0) - jnp.log1p(-s0)
    sc_conv_w = nrm(ks[19], (DEPTH, SC_CONV_W, SC_WIDTH), SC_CONV_W ** -0.5)
    w_branch_attn = nrm(ks[20], (DEPTH, ATT_WIDTH, D_MODEL), ATT_WIDTH ** -0.5)
    w_branch_rg = nrm(ks[21], (DEPTH, LRU_WIDTH, D_MODEL), LRU_WIDTH ** -0.5)
    w_branch_sc = nrm(ks[22], (DEPTH, SC_WIDTH, D_MODEL), SC_WIDTH ** -0.5)
    w_out = nrm(ks[23], (DEPTH, D_MODEL, D_MODEL), D_MODEL ** -0.5)
    return {'x_prompt': x_prompt, 'x_sample': x_sample, 'cache_k': cache_k, 'cache_v': cache_v,
            'page_table': page_table, 'state_rglru_h': state_rglru_h, 'state_rglru_conv': state_rglru_conv,
            'state_sconv': state_sconv, 'norm_g': norm_g, 'w_ffn_up': w_ffn_up, 'w_ffn_down': w_ffn_down,
            'w_in': w_in, 'lambda_qk': lambda_qk, 'subln_g': subln_g, 'rg_conv_w': rg_conv_w,
            'rg_conv_b': rg_conv_b, 'rg_gate_w': rg_gate_w, 'rg_gate_b': rg_gate_b, 'rg_lambda': rg_lambda,
            'sc_conv_w': sc_conv_w, 'w_branch_attn': w_branch_attn, 'w_branch_rg': w_branch_rg,
            'w_branch_sc': w_branch_sc, 'w_out': w_out}


def reference(x_prompt, x_sample, cache_k, cache_v, page_table, state_rglru_h, state_rglru_conv, state_sconv,
              norm_g, w_ffn_up, w_ffn_down, w_in, lambda_qk, subln_g, rg_conv_w, rg_conv_b, rg_gate_w,
              rg_gate_b, rg_lambda, sc_conv_w, w_branch_attn, w_branch_rg, w_branch_sc, w_out):
    p = dict(norm_g=norm_g, w_ffn_up=w_ffn_up, w_ffn_down=w_ffn_down, w_in=w_in, lambda_qk=lambda_qk,
             subln_g=subln_g, rg_conv_w=rg_conv_w, rg_conv_b=rg_conv_b, rg_gate_w=rg_gate_w,
             rg_gate_b=rg_gate_b, rg_lambda=rg_lambda, sc_conv_w=sc_conv_w, w_branch_attn=w_branch_attn,
             w_branch_rg=w_branch_rg, w_branch_sc=w_branch_sc, w_out=w_out)
    bp = x_prompt.shape[0]
    yp, ys = x_prompt, x_sample
    prompt_states, sample_states = [], []
    for l in range(DEPTH):
        yp, st_p = _trunk_layer(yp, l, _diff_attn_prompt,
                                jnp.zeros((bp, LRU_WIDTH), jnp.float32),
                                jnp.zeros((bp, LRU_CONV_W - 1, LRU_WIDTH), jnp.float32),
                                jnp.zeros((bp, SC_CONV_W - 1, SC_WIDTH), jnp.float32), p)
        prompt_states.append(st_p)
        attn_s = functools.partial(_diff_attn_paged, cache_k=cache_k, cache_v=cache_v, layer=l,
                                   page_table=page_table)
        ys, st_s = _trunk_layer(ys, l, attn_s, state_rglru_h[l], state_rglru_conv[l], state_sconv[l], p)
        sample_states.append(st_s)
    k_p, v_p, h_p, rgc_p, sc_p = [jnp.stack(s, axis=0) for s in zip(*prompt_states)]
    k_s, v_s, h_s, rgc_s, sc_s = [jnp.stack(s, axis=0) for s in zip(*sample_states)]
    return (yp, ys, k_p, v_p, h_p, rgc_p, sc_p, k_s, v_s, h_s, rgc_s, sc_s)
```

```python
import functools
import math

import jax
import jax.numpy as jnp
from jax import lax
from jax.experimental import pallas as pl
from jax.experimental.pallas import tpu as pltpu

F32 = jnp.float32
BF16 = jnp.bfloat16

NORM_EPS = 1e-6
LRU_C = 8.0
N_HEADS = 8
N_KV_HEADS = 4
HEAD_GROUP = N_HEADS // N_KV_HEADS
LRU_BLOCKS = 8
N_BRANCHES = 3
MASK_VALUE = -1e30
VMEM_LIMIT_BYTES = 56 * 1024 * 1024


def _cparams(*sem):
    return pltpu.CompilerParams(dimension_semantics=sem, vmem_limit_bytes=VMEM_LIMIT_BYTES)


def _rms(x, g):
    return x * lax.rsqrt(jnp.mean(x * x, axis=-1, keepdims=True) + NORM_EPS) * g


def _dot(a, b):
    return jnp.dot(a, b, preferred_element_type=F32)


def _dot_nt(a, b):
    return lax.dot_general(a, b, (((1,), (1,)), ((), ())), preferred_element_type=F32)


def _sigmoid(x):
    return 1.0 / (1.0 + jnp.exp(-x))


def _gelu_tanh(x):
    c = math.sqrt(2.0 / math.pi)
    return 0.5 * x * (1.0 + jnp.tanh(c * (x + 0.044715 * (x * x * x))))


def _resident(shape, index_map):
    return pl.BlockSpec(shape, index_map, pipeline_mode=pl.Buffered(1))


def _ffn_kernel(x_ref, g_ref, wup_ref, wdn_ref, o_ref, *, d_ff, chunks, g_pre, g_post):
    x = x_ref[...]
    h = _rms(x, g_ref[g_pre:g_pre + 1, :]).astype(BF16)
    acc = None
    for c0, cw in chunks:
        gate = _dot(h, wup_ref[:, c0:c0 + cw])
        up = _dot(h, wup_ref[:, d_ff + c0:d_ff + c0 + cw])
        act = (gate * _sigmoid(gate) * up).astype(BF16)
        part = _dot(act, wdn_ref[c0:c0 + cw, :])
        acc = part if acc is None else acc + part
    o_ref[...] = x + 0.5 * _rms(acc, g_ref[g_post:g_post + 1, :])


def _ffn(x, norm_g, w_up, w_dn, layer, which, tm):
    rows, d = x.shape
    d_ff = w_dn.shape[2]
    chunk = 1024
    chunks = tuple((c0, min(chunk, d_ff - c0)) for c0 in range(0, d_ff, chunk))
    kern = functools.partial(_ffn_kernel, d_ff=d_ff, chunks=chunks,
                             g_pre=0 if which == 0 else 4, g_post=1 if which == 0 else 5)
    return pl.pallas_call(
        kern,
        grid=(rows // tm,),
        in_specs=[
            pl.BlockSpec((tm, d), lambda i: (i, 0)),
            _resident((None, 6, d), lambda i: (layer, 0, 0)),
            _resident((None, None, d, 2 * d_ff), lambda i: (layer, which, 0, 0)),
            _resident((None, None, d_ff, d), lambda i: (layer, which, 0, 0)),
        ],
        out_specs=pl.BlockSpec((tm, d), lambda i: (i, 0)),
        out_shape=jax.ShapeDtypeStruct((rows, d), F32),
        compiler_params=_cparams("parallel"),
        name=f"ffn{which}",
    )(x, norm_g, w_up, w_dn)


def _qkv_kernel(x_ref, g_ref, w_ref, q_ref, k_ref, v_ref, *, qw, kw, scale):
    h = _rms(x_ref[...], g_ref[2:3, :]).astype(BF16)
    z = _dot(h, w_ref[...])
    q_ref[...] = (z[:, :qw] * scale).astype(BF16)
    k_ref[...] = z[:, qw:qw + kw]
    v_ref[...] = z[:, qw + kw:]


def _zrest_kernel(x_ref, g_ref, w_ref, z_ref):
    h = _rms(x_ref[...], g_ref[2:3, :]).astype(BF16)
    z_ref[...] = _dot(h, w_ref[...])


def _inproj(x, norm_g, w_in, layer, tm, qw, kw, scale):
    rows, d = x.shape
    in_w = w_in.shape[2]
    cw = qw + 2 * kw
    n_rest = (in_w - cw) // cw
    assert cw * (n_rest + 1) == in_w
    q, k, v = pl.pallas_call(
        functools.partial(_qkv_kernel, qw=qw, kw=kw, scale=scale),
        grid=(rows // tm,),
        in_specs=[
            pl.BlockSpec((tm, d), lambda i: (i, 0)),
            _resident((None, 6, d), lambda i: (layer, 0, 0)),
            _resident((None, d, cw), lambda i: (layer, 0, 0)),
        ],
        out_specs=[
            pl.BlockSpec((tm, qw), lambda i: (i, 0)),
            pl.BlockSpec((tm, kw), lambda i: (i, 0)),
            pl.BlockSpec((tm, kw), lambda i: (i, 0)),
        ],
        out_shape=[
            jax.ShapeDtypeStruct((rows, qw), BF16),
            jax.ShapeDtypeStruct((rows, kw), F32),
            jax.ShapeDtypeStruct((rows, kw), F32),
        ],
        compiler_params=_cparams("parallel"),
        name="inproj_qkv",
    )(x, norm_g, w_in)
    zr = pl.pallas_call(
        _zrest_kernel,
        grid=(n_rest, rows // tm),
        in_specs=[
            pl.BlockSpec((tm, d), lambda j, i: (i, 0)),
            _resident((None, 6, d), lambda j, i: (layer, 0, 0)),
            pl.BlockSpec((None, d, cw), lambda j, i: (layer, 0, j + 1)),
        ],
        out_specs=pl.BlockSpec((tm, cw), lambda j, i: (i, j)),
        out_shape=jax.ShapeDtypeStruct((rows, in_w - cw), F32),
        compiler_params=_cparams("parallel", "parallel"),
        name="inproj_rest",
    )(x, norm_g, w_in)
    return q, k, v, zr


def _lambda_full(lqk_ref, lam_init):
    s01 = jnp.sum(lqk_ref[0:1, :] * lqk_ref[1:2, :], axis=-1, keepdims=True)
    s23 = jnp.sum(lqk_ref[2:3, :] * lqk_ref[3:4, :], axis=-1, keepdims=True)
    return jnp.exp(s01) - jnp.exp(s23) + lam_init


def _subln(o, sg, lam_init):
    return _rms(o, sg) * (1.0 - lam_init)


def _attn_kernel(slopes_ref, q_ref, k_ref, v_ref, lqk_ref, sg_ref, o_ref,
                 m_ref, l_ref, acc_ref, *, tq, tk, lam_init):
    kvh = pl.program_id(1)
    i = pl.program_id(2)
    hd2 = k_ref.shape[1]
    q = q_ref[...]
    qs = jnp.concatenate([q[:, :hd2], q[:, hd2:]], axis=0)
    lane = lax.broadcasted_iota(jnp.int32, (1, hd2), 1)
    zero = jnp.zeros_like(qs)
    qm = (jnp.where(lane < hd2 // 2, qs, zero), jnp.where(lane < hd2 // 2, zero, qs))
    row = lax.broadcasted_iota(jnp.int32, (2 * tq, 1), 0)
    slope = jnp.where(row < tq, slopes_ref[2 * kvh], slopes_ref[2 * kvh + 1])
    qrow = i * tq + jnp.where(row < tq, row, row - tq)

    m_ref[...] = jnp.full(m_ref.shape, MASK_VALUE, F32)
    l_ref[...] = jnp.zeros(l_ref.shape, F32)
    acc_ref[...] = jnp.zeros(acc_ref.shape, F32)

    def step(j, masked):
        k0 = pl.multiple_of(j * tk, tk)
        kb = k_ref[pl.ds(k0, tk), :].astype(BF16)
        vb = v_ref[pl.ds(k0, tk), :].astype(BF16)
        kpos = k0 + lax.broadcasted_iota(jnp.int32, (1, tk), 1)
        bias = slope * kpos.astype(F32)
        for mm in range(2):
            s = _dot_nt(qm[mm], kb) + bias
            if masked:
                s = jnp.where(kpos <= qrow, s, MASK_VALUE)
            m_old = m_ref[mm]
            m_new = jnp.maximum(m_old, jnp.max(s, axis=-1, keepdims=True))
            p = jnp.exp(s - m_new)
            corr = jnp.exp(m_old - m_new)
            l_ref[mm] = corr * l_ref[mm] + jnp.sum(p, axis=-1, keepdims=True)
            acc_ref[mm] = corr * acc_ref[mm] + _dot(p.astype(BF16), vb)
            m_ref[mm] = m_new

    n_full = (i * tq) // tk
    n_diag = tq // tk

    def body(j, carry):
        step(j, False)
        return carry

    lax.fori_loop(0, n_full, body, 0)
    for d in range(n_diag):
        step(n_full + d, True)

    lam = _lambda_full(lqk_ref, lam_init)
    o = acc_ref[0] / l_ref[0] - lam * (acc_ref[1] / l_ref[1])
    o = _subln(o, sg_ref[...], lam_init).astype(o_ref.dtype)
    o_ref[:, :hd2] = o[:tq]
    o_ref[:, hd2:] = o[tq:]


def _attn_prompt(q, k, v, slopes, lambda_qk, subln_g, layer, batch, seq, tq, tk, lam_init):
    rows, qw = q.shape
    hd2 = k.shape[1] // N_KV_HEADS
    nq = seq // tq
    kern = functools.partial(_attn_kernel, tq=tq, tk=tk, lam_init=lam_init)
    return pl.pallas_call(
        kern,
        grid_spec=pltpu.PrefetchScalarGridSpec(
            num_scalar_prefetch=1,
            grid=(batch, N_KV_HEADS, nq),
            in_specs=[
                pl.BlockSpec((tq, HEAD_GROUP * hd2), lambda b, h, i, s: (b * nq + i, h)),
                pl.BlockSpec((seq, hd2), lambda b, h, i, s: (b, h)),
                pl.BlockSpec((seq, hd2), lambda b, h, i, s: (b, h)),
                pl.BlockSpec((None, 4, hd2 // 2), lambda b, h, i, s: (layer, 0, 0)),
                pl.BlockSpec((None, 1, hd2), lambda b, h, i, s: (layer, 0, 0)),
            ],
            out_specs=pl.BlockSpec((tq, HEAD_GROUP * hd2), lambda b, h, i, s: (b * nq + i, h)),
            scratch_shapes=[
                pltpu.VMEM((2, 2 * tq, 1), F32),
                pltpu.VMEM((2, 2 * tq, 1), F32),
                pltpu.VMEM((2, 2 * tq, hd2), F32),
            ],
        ),
        out_shape=jax.ShapeDtypeStruct((rows, qw), BF16),
        compiler_params=_cparams("parallel", "parallel", "parallel"),
        name="attn_prompt",
    )(slopes, q, k, v, lambda_qk, subln_g)


def _paged_kernel(pt_ref, slopes_ref, qt_ref, ks_ref, vs_ref, lqk_ref, sg_ref, *refs,
                  pages_per_step, page, past, lam_init):
    del pt_ref
    pp = pages_per_step
    k_refs = refs[:pp]
    v_refs = refs[pp:2 * pp]
    o_ref = refs[2 * pp]
    m_ref, l_ref, acc_ref = refs[2 * pp + 1:]
    c = pl.program_id(1)
    nrow = qt_ref.shape[0]
    hd2 = sg_ref.shape[1]
    qt = qt_ref[...]

    @pl.when(c == 0)
    def _():
        ks = ks_ref[...].astype(BF16).astype(F32)
        m_ref[...] = jnp.sum(qt.astype(F32) * ks, axis=-1, keepdims=True)
        l_ref[...] = jnp.ones(l_ref.shape, F32)
        acc_ref[...] = jnp.broadcast_to(vs_ref[...].astype(BF16).astype(F32), acc_ref.shape)

    row = lax.broadcasted_iota(jnp.int32, (nrow, 1), 0)
    head = row % N_HEADS
    slope = jnp.zeros((nrow, 1), F32)
    for h in range(N_HEADS):
        slope = jnp.where(head == h, slopes_ref[h], slope)
    s_parts = []
    for r in range(pp):
        kpos = (c * pp + r) * page + lax.broadcasted_iota(jnp.int32, (1, page), 1)
        dist = (past - kpos).astype(F32)
        s_parts.append(_dot_nt(qt, k_refs[r][...].astype(BF16)) - slope * dist)
    s = jnp.concatenate(s_parts, axis=-1)
    m_old = m_ref[...]
    m_new = jnp.maximum(m_old, jnp.max(s, axis=-1, keepdims=True))
    p = jnp.exp(s - m_new)
    corr = jnp.exp(m_old - m_new)
    l_ref[...] = corr * l_ref[...] + jnp.sum(p, axis=-1, keepdims=True)
    pb = p.astype(BF16)
    pv = None
    for r in range(pp):
        part = _dot(pb[:, r * page:(r + 1) * page], v_refs[r][...].astype(BF16))
        pv = part if pv is None else pv + part
    acc_ref[...] = corr * acc_ref[...] + pv
    m_ref[...] = m_new

    @pl.when(c == pl.num_programs(1) - 1)
    def _():
        lam = _lambda_full(lqk_ref, lam_init)
        a = acc_ref[...] / l_ref[...]
        o = a[:N_HEADS] - lam * a[N_HEADS:]
        hrow = lax.broadcasted_iota(jnp.int32, (N_HEADS, 1), 0) // HEAD_GROUP
        sel = jnp.zeros((N_HEADS, hd2), F32)
        for kv in range(N_KV_HEADS):
            sel = jnp.where(hrow == kv, o[:, kv * hd2:(kv + 1) * hd2], sel)
        o_ref[...] = _subln(sel, sg_ref[...], lam_init).astype(o_ref.dtype)


def _attn_paged(qt, k_new, v_new, cache_k, cache_v, page_table, slopes, lambda_qk, subln_g,
                layer, pages_per_step, lam_init):
    nb, nrow, kvw = qt.shape
    page = cache_k.shape[2]
    n_pages = page_table.shape[1]
    pp = pages_per_step
    hd2 = kvw // N_KV_HEADS
    kern = functools.partial(_paged_kernel, pages_per_step=pp, page=page,
                             past=n_pages * page, lam_init=lam_init)

    def page_spec(r):
        return pl.BlockSpec((None, None, page, kvw),
                            lambda b, c, pt, s: (layer, pt[b, c * pp + r], 0, 0))

    return pl.pallas_call(
        kern,
        grid_spec=pltpu.PrefetchScalarGridSpec(
            num_scalar_prefetch=2,
            grid=(nb, n_pages // pp),
            in_specs=[
                pl.BlockSpec((None, nrow, kvw), lambda b, c, pt, s: (b, 0, 0)),
                pl.BlockSpec((None, 1, kvw), lambda b, c, pt, s: (b, 0, 0)),
                pl.BlockSpec((None, 1, kvw), lambda b, c, pt, s: (b, 0, 0)),
                pl.BlockSpec((None, 4, hd2 // 2), lambda b, c, pt, s: (layer, 0, 0)),
                pl.BlockSpec((None, 1, hd2), lambda b, c, pt, s: (layer, 0, 0)),
            ] + [page_spec(r) for r in range(pp)] + [page_spec(r) for r in range(pp)],
            out_specs=pl.BlockSpec((None, N_HEADS, hd2), lambda b, c, pt, s: (b, 0, 0)),
            scratch_shapes=[
                pltpu.VMEM((nrow, 1), F32),
                pltpu.VMEM((nrow, 1), F32),
                pltpu.VMEM((nrow, kvw), F32),
            ],
        ),
        out_shape=jax.ShapeDtypeStruct((nb, N_HEADS, hd2), BF16),
        compiler_params=_cparams("parallel", "arbitrary"),
        name="attn_paged",
    )(page_table, slopes, qt, k_new, v_new, lambda_qk, subln_g,
      *([cache_k] * pp), *([cache_v] * pp))


def _lru_gates(xc, gw_ref, gb_ref, lam_ref):
    w = xc.shape[1]
    bw = w // LRU_BLOCKS
    g0, g1 = [], []
    for n in range(LRU_BLOCKS):
        xb = xc[:, n * bw:(n + 1) * bw].astype(BF16)
        g0.append(_dot(xb, gw_ref[0, n]))
        g1.append(_dot(xb, gw_ref[1, n]))
    r = _sigmoid(jnp.concatenate(g0, axis=-1) + gb_ref[0:1, :])
    i = _sigmoid(jnp.concatenate(g1, axis=-1) + gb_ref[1:2, :])
    neg_lam = -lam_ref[...]
    softplus = jnp.maximum(neg_lam, 0.0) + jnp.log1p(jnp.exp(-jnp.abs(neg_lam)))
    log_a = -LRU_C * r * softplus
    a = jnp.exp(log_a)
    u = jnp.sqrt(1.0 - a * a) * (i * xc)
    return a, u


def _scan_rows(a, u):
    rows = a.shape[0]
    ridx = lax.broadcasted_iota(jnp.int32, (rows, 1), 0)
    s = 1
    while s < rows:
        if s < 8:
            keep = ridx >= s
            a_sh = jnp.where(keep, pltpu.roll(a, s, 0), 1.0)
            u_sh = jnp.where(keep, pltpu.roll(u, s, 0), 0.0)
        else:
            a_sh = jnp.concatenate([jnp.ones((s, a.shape[1]), F32), a[:rows - s]], axis=0)
            u_sh = jnp.concatenate([jnp.zeros((s, a.shape[1]), F32), u[:rows - s]], axis=0)
        u = a * u_sh + u
        a = a * a_sh
        s *= 2
    return a, u


def _mixer_kernel(rgx_ref, rgg_ref, scb_ref, scc_ref, scx_ref, h0_ref, rgbuf_ref, scbuf_ref,
                  cw_ref, cb_ref, gw_ref, gb_ref, lam_ref, scw_ref,
                  yrg_ref, ysc_ref, hout_ref, rgbuf_out_ref, scbuf_out_ref,
                  xp_rg, xp_sc, hcarry, *, tt):
    t = pl.program_id(1)
    nrg = rgbuf_ref.shape[0]
    nsc = scbuf_ref.shape[0]

    @pl.when(t == 0)
    def _():
        xp_rg[8 - nrg:8, :] = rgbuf_ref[...]
        xp_sc[8 - nsc:8, :] = scbuf_ref[...]
        hcarry[...] = h0_ref[...]

    xp_rg[8:8 + tt, :] = rgx_ref[...]
    xc = cb_ref[...] + cw_ref[0:1, :] * xp_rg[8 - nrg:8 - nrg + tt, :]
    for j in range(1, nrg + 1):
        xc = xc + cw_ref[j:j + 1, :] * xp_rg[8 - nrg + j:8 - nrg + j + tt, :]
    a, u = _lru_gates(xc, gw_ref, gb_ref, lam_ref)
    a_cum, h_loc = _scan_rows(a, u)
    h = h_loc + a_cum * hcarry[...]
    hcarry[...] = h[tt - 1:tt, :]
    yrg_ref[...] = (h * _gelu_tanh(rgg_ref[...])).astype(yrg_ref.dtype)
    xp_rg[8 - nrg:8, :] = xp_rg[8 + tt - nrg:8 + tt, :]

    xp_sc[8:8 + tt, :] = scc_ref[...] * scx_ref[...]
    y = scw_ref[0:1, :] * xp_sc[8 - nsc:8 - nsc + tt, :]
    for j in range(1, nsc + 1):
        y = y + scw_ref[j:j + 1, :] * xp_sc[8 - nsc + j:8 - nsc + j + tt, :]
    ysc_ref[...] = (scb_ref[...] * y).astype(ysc_ref.dtype)
    xp_sc[8 - nsc:8, :] = xp_sc[8 + tt - nsc:8 + tt, :]

    @pl.when(t == pl.num_programs(1) - 1)
    def _():
        hout_ref[...] = hcarry[...]
        rgbuf_out_ref[...] = xp_rg[8 - nrg:8, :]
        scbuf_out_ref[...] = xp_sc[8 - nsc:8, :]


def _mixer_prompt(zr, h0, rg_buf, sc_buf, rg_conv_w, rg_conv_b, gate_w, rg_gate_b, rg_lambda,
                  sc_conv_w, layer, batch, seq, tt):
    rows = zr.shape[0]
    w = h0.shape[-1]
    nt = seq // tt
    nrg, nsc = rg_buf.shape[1], sc_buf.shape[1]

    def zcol(cidx):
        return pl.BlockSpec((tt, w), lambda b, t: (b * nt + t, cidx))

    def per_batch(n):
        return pl.BlockSpec((None, n, w), lambda b, t: (b, 0, 0))

    def per_layer(n):
        return pl.BlockSpec((None, n, w), lambda b, t: (layer, 0, 0))

    bw = w // LRU_BLOCKS
    return pl.pallas_call(
        functools.partial(_mixer_kernel, tt=tt),
        grid=(batch, nt),
        in_specs=[zcol(0), zcol(1), zcol(2), zcol(3), zcol(4),
                  per_batch(1), per_batch(nrg), per_batch(nsc),
                  per_layer(nrg + 1), per_layer(1),
                  pl.BlockSpec((None, 2, LRU_BLOCKS, bw, bw), lambda b, t: (layer, 0, 0, 0, 0)),
                  per_layer(2), per_layer(1), per_layer(nsc + 1)],
        out_specs=[pl.BlockSpec((tt, w), lambda b, t: (b * nt + t, 0)),
                   pl.BlockSpec((tt, w), lambda b, t: (b * nt + t, 0)),
                   per_batch(1), per_batch(nrg), per_batch(nsc)],
        out_shape=[jax.ShapeDtypeStruct((rows, w), BF16),
                   jax.ShapeDtypeStruct((rows, w), BF16),
                   jax.ShapeDtypeStruct((batch, 1, w), F32),
                   jax.ShapeDtypeStruct((batch, nrg, w), F32),
                   jax.ShapeDtypeStruct((batch, nsc, w), F32)],
        scratch_shapes=[pltpu.VMEM((8 + tt, w), F32), pltpu.VMEM((8 + tt, w), F32),
                        pltpu.VMEM((1, w), F32)],
        compiler_params=_cparams("parallel", "arbitrary"),
        name="mixer_prompt",
    )(zr, zr, zr, zr, zr, h0, rg_buf, sc_buf, rg_conv_w, rg_conv_b, gate_w, rg_gate_b,
      rg_lambda, sc_conv_w)


def _mixer_step_kernel(rgx_ref, rgg_ref, scb_ref, scc_ref, scx_ref, h0_ref, rgbuf_ref, scbuf_ref,
                       cw_ref, cb_ref, gw_ref, gb_ref, lam_ref, scw_ref,
                       yrg_ref, ysc_ref, hout_ref, rgbuf_out_ref, scbuf_out_ref, *, nrg, nsc):
    w = h0_ref.shape[1]
    x = rgx_ref[...]
    xc = cb_ref[...] + cw_ref[nrg:nrg + 1, :] * x
    for j in range(nrg):
        xc = xc + cw_ref[j:j + 1, :] * rgbuf_ref[:, j * w:(j + 1) * w]
    a, u = _lru_gates(xc, gw_ref, gb_ref, lam_ref)
    h = a * h0_ref[...] + u
    hout_ref[...] = h
    yrg_ref[...] = (h * _gelu_tanh(rgg_ref[...])).astype(yrg_ref.dtype)
    for j in range(nrg - 1):
        rgbuf_out_ref[:, j * w:(j + 1) * w] = rgbuf_ref[:, (j + 1) * w:(j + 2) * w]
    rgbuf_out_ref[:, (nrg - 1) * w:] = x

    cx = scc_ref[...] * scx_ref[...]
    y = scw_ref[nsc:nsc + 1, :] * cx
    for j in range(nsc):
        y = y + scw_ref[j:j + 1, :] * scbuf_ref[:, j * w:(j + 1) * w]
    ysc_ref[...] = (scb_ref[...] * y).astype(ysc_ref.dtype)
    for j in range(nsc - 1):
        scbuf_out_ref[:, j * w:(j + 1) * w] = scbuf_ref[:, (j + 1) * w:(j + 2) * w]
    scbuf_out_ref[:, (nsc - 1) * w:] = cx


def _mixer_sample(zr, h0, rg_buf, sc_buf, rg_conv_w, rg_conv_b, gate_w, rg_gate_b, rg_lambda,
                  sc_conv_w, layer):
    nb, w = h0.shape
    nrg, nsc = rg_buf.shape[1], sc_buf.shape[1]
    bw = w // LRU_BLOCKS

    def zcol(cidx):
        return pl.BlockSpec((nb, w), lambda i: (0, cidx))

    def full(n):
        return pl.BlockSpec((nb, n * w), lambda i: (0, 0))

    def per_layer(n):
        return pl.BlockSpec((None, n, w), lambda i: (layer, 0, 0))

    outs = pl.pallas_call(
        functools.partial(_mixer_step_kernel, nrg=nrg, nsc=nsc),
        grid=(1,),
        in_specs=[zcol(0), zcol(1), zcol(2), zcol(3), zcol(4),
                  full(1), full(nrg), full(nsc),
                  per_layer(nrg + 1), per_layer(1),
                  pl.BlockSpec((None, 2, LRU_BLOCKS, bw, bw), lambda i: (layer, 0, 0, 0, 0)),
                  per_layer(2), per_layer(1), per_layer(nsc + 1)],
        out_specs=[full(1), full(1), full(1), full(nrg), full(nsc)],
        out_shape=[jax.ShapeDtypeStruct((nb, w), BF16),
                   jax.ShapeDtypeStruct((nb, w), BF16),
                   jax.ShapeDtypeStruct((nb, w), F32),
                   jax.ShapeDtypeStruct((nb, nrg * w), F32),
                   jax.ShapeDtypeStruct((nb, nsc * w), F32)],
        compiler_params=_cparams("arbitrary"),
        name="mixer_sample",
    )(zr, zr, zr, zr, zr, h0, rg_buf.reshape(nb, nrg * w), sc_buf.reshape(nb, nsc * w),
      rg_conv_w, rg_conv_b, gate_w, rg_gate_b, rg_lambda, sc_conv_w)
    y_rg, y_sc, h, rgb, scb = outs
    return y_rg, y_sc, h, rgb.reshape(nb, nrg, w), scb.reshape(nb, nsc, w)


def _merge_kernel(x_ref, o_ref, yrg_ref, ysc_ref, gz0_ref, gz1_ref, gz2_ref, g_ref,
                  wa_ref, wr_ref, ws_ref, wo_ref, out_ref):
    m = _sigmoid(gz0_ref[...]) * _dot(o_ref[...], wa_ref[...])
    m = m + _sigmoid(gz1_ref[...]) * _dot(yrg_ref[...], wr_ref[...])
    m = m + _sigmoid(gz2_ref[...]) * _dot(ysc_ref[...], ws_ref[...])
    y = _dot(m.astype(BF16), wo_ref[...])
    out_ref[...] = x_ref[...] + _rms(y, g_ref[3:4, :])


def _merge(x, o, y_rg, y_sc, zr, norm_g, w_a, w_r, w_s, w_o, layer, tm, gz_col):
    rows, d = x.shape

    def rowblk(cidx=0):
        return pl.BlockSpec((tm, d), lambda i: (i, cidx))

    def wspec(arr):
        return _resident((None,) + arr.shape[1:], lambda i: (layer, 0, 0))

    return pl.pallas_call(
        _merge_kernel,
        grid=(rows // tm,),
        in_specs=[rowblk(), rowblk(), rowblk(), rowblk(),
                  rowblk(gz_col), rowblk(gz_col + 1), rowblk(gz_col + 2),
                  _resident((None, 6, d), lambda i: (layer, 0, 0)),
                  wspec(w_a), wspec(w_r), wspec(w_s), wspec(w_o)],
        out_specs=rowblk(),
        out_shape=jax.ShapeDtypeStruct((rows, d), F32),
        compiler_params=_cparams("parallel"),
        name="merge",
    )(x, o, y_rg, y_sc, zr, zr, zr, norm_g, w_a, w_r, w_s, w_o)


def _pick_tile(n, pref):
    t = min(n, pref)
    while n % t:
        t //= 2
    return t


def kernel(x_prompt, x_sample, cache_k, cache_v, page_table, state_rglru_h, state_rglru_conv, state_sconv, norm_g, w_ffn_up, w_ffn_down, w_in, lambda_qk, subln_g, rg_conv_w, rg_conv_b, rg_gate_w, rg_gate_b, rg_lambda, sc_conv_w, w_branch_attn, w_branch_rg, w_branch_sc, w_out):
    batch, seq, d = x_prompt.shape
    nb, dec_seq, _ = x_sample.shape
    assert dec_seq == 1, "the sample group carries one new token per sequence"
    depth = w_in.shape[0]
    n_pool, page = cache_k.shape[1], cache_k.shape[2]
    head_dim = cache_k.shape[-1]
    hd2 = 2 * head_dim
    qw = N_HEADS * hd2
    kw = N_KV_HEADS * hd2
    w = state_rglru_h.shape[-1]
    assert qw == d and w == d and qw + 2 * kw == 2 * d
    scale = head_dim ** -0.5
    assert math.log2(scale) == round(math.log2(scale)), "q pre-scaling must be exact in bf16"
    n_pages = page_table.shape[1]
    nrg, nsc = state_rglru_conv.shape[2], state_sconv.shape[2]

    w_up_b = w_ffn_up.astype(BF16)
    w_dn_b = w_ffn_down.astype(BF16)
    w_in_b = w_in.astype(BF16)
    w_a_b = w_branch_attn.astype(BF16)
    w_r_b = w_branch_rg.astype(BF16)
    w_s_b = w_branch_sc.astype(BF16)
    w_o_b = w_out.astype(BF16)
    gate_w_b = rg_gate_w.astype(BF16)

    slopes = 2.0 ** (-8.0 * jnp.arange(1, N_HEADS + 1, dtype=F32) / N_HEADS)
    cache_k4 = cache_k.reshape(depth, n_pool, page, kw)
    cache_v4 = cache_v.reshape(depth, n_pool, page, kw)
    rg_conv_b3 = rg_conv_b.reshape(depth, 1, w)
    rg_lambda3 = rg_lambda.reshape(depth, 1, w)
    subln_g3 = subln_g.reshape(depth, 1, hd2)
    eye_kv = jnp.eye(N_KV_HEADS, dtype=BF16)
    eye_m = jnp.eye(2, dtype=BF16)

    tm = _pick_tile(batch * seq, 512)
    tq = _pick_tile(seq, 256)
    tt = _pick_tile(seq, 256)
    pps = _pick_tile(n_pages, 8)
    gz_col = 5

    xp = x_prompt.reshape(batch * seq, d)
    xs = x_sample.reshape(nb, d)
    zeros_h = jnp.zeros((batch, 1, w), F32)
    zeros_rg = jnp.zeros((batch, nrg, w), F32)
    zeros_sc = jnp.zeros((batch, nsc, w), F32)

    p_states, s_states = [], []
    for l in range(depth):
        lam_init = 0.8 - 0.6 * math.exp(-0.3 * l)
        dense = dict(norm_g=norm_g, layer=l)

        xp = _ffn(xp, norm_g, w_up_b, w_dn_b, l, 0, tm)
        q, k, v, zr = _inproj(xp, norm_g, w_in_b, l, tm, qw, kw, scale)
        o = _attn_prompt(q, k, v, slopes, lambda_qk, subln_g3, l, batch, seq, tq, tq, lam_init)
        y_rg, y_sc, h_p, rgb_p, scb_p = _mixer_prompt(
            zr, zeros_h, zeros_rg, zeros_sc, rg_conv_w, rg_conv_b3, gate_w_b, rg_gate_b,
            rg_lambda3, sc_conv_w, l, batch, seq, tt)
        xp = _merge(xp, o, y_rg, y_sc, zr, norm_g, w_a_b, w_r_b, w_s_b, w_o_b, l, tm, gz_col)
        xp = _ffn(xp, norm_g, w_up_b, w_dn_b, l, 1, tm)
        p_states.append((k.reshape(batch, seq, N_KV_HEADS, 2, head_dim),
                         v.reshape(batch, seq, N_KV_HEADS, hd2),
                         h_p.reshape(batch, w), rgb_p, scb_p))

        xs = _ffn(xs, norm_g, w_up_b, w_dn_b, l, 0, nb)
        q, k, v, zr = _inproj(xs, norm_g, w_in_b, l, nb, qw, kw, scale)
        q5 = q.reshape(nb, N_KV_HEADS, HEAD_GROUP, 2, head_dim)
        qt = jnp.einsum('bkgmd,kK,mM->bmkgKMd', q5, eye_kv, eye_m).reshape(nb, 2 * N_HEADS, kw)
        o = _attn_paged(qt, k.reshape(nb, 1, kw), v.reshape(nb, 1, kw), cache_k4, cache_v4,
                        page_table, slopes, lambda_qk, subln_g3, l, pps, lam_init)
        o = o.reshape(nb, qw)
        y_rg, y_sc, h_s, rgb_s, scb_s = _mixer_sample(
            zr, state_rglru_h[l], state_rglru_conv[l], state_sconv[l], rg_conv_w, rg_conv_b3,
            gate_w_b, rg_gate_b, rg_lambda3, sc_conv_w, l)
        xs = _merge(xs, o, y_rg, y_sc, zr, norm_g, w_a_b, w_r_b, w_s_b, w_o_b, l, nb, gz_col)
        xs = _ffn(xs, norm_g, w_up_b, w_dn_b, l, 1, nb)
        s_states.append((k.reshape(nb, 1, N_KV_HEADS, 2, head_dim),
                         v.reshape(nb, 1, N_KV_HEADS, hd2), h_s, rgb_s, scb_s))

    k_p, v_p, h_p, rgc_p, sc_p = [jnp.stack(s, axis=0) for s in zip(*p_states)]
    k_s, v_s, h_s, rgc_s, sc_s = [jnp.stack(s, axis=0) for s in zip(*s_states)]
    return (xp.reshape(batch, seq, d), xs.reshape(nb, 1, d), k_p, v_p, h_p, rgc_p, sc_p,
            k_s, v_s, h_s, rgc_s, sc_s)
```

```python
import functools
import math

import numpy as np
import jax
import jax.numpy as jnp
from jax import lax
from jax.experimental import pallas as pl
from jax.experimental.pallas import tpu as pltpu

F32 = jnp.float32
BF16 = jnp.bfloat16

NORM_EPS = 1e-6
LRU_C = 8.0
N_HEADS = 8
N_KV_HEADS = 4
HEAD_GROUP = N_HEADS // N_KV_HEADS
LRU_BLOCKS = 8
N_BRANCHES = 3
MASK_VALUE = -1e30
VMEM_LIMIT_BYTES = 56 * 1024 * 1024


def _cparams(*sem):
    return pltpu.CompilerParams(dimension_semantics=sem, vmem_limit_bytes=VMEM_LIMIT_BYTES)


def _rms(x, g):
    return x * lax.rsqrt(jnp.mean(x * x, axis=-1, keepdims=True) + NORM_EPS) * g


def _dot(a, b):
    return jnp.dot(a, b, preferred_element_type=F32)


def _dot_nt(a, b):
    return lax.dot_general(a, b, (((1,), (1,)), ((), ())), preferred_element_type=F32)


def _sigmoid(x):
    return 1.0 / (1.0 + jnp.exp(-x))


def _gelu_tanh(x):
    c = math.sqrt(2.0 / math.pi)
    return 0.5 * x * (1.0 + jnp.tanh(c * (x + 0.044715 * (x * x * x))))


def _resident(shape, index_map):
    return pl.BlockSpec(shape, index_map, pipeline_mode=pl.Buffered(1))


def _ffn_kernel(x_ref, g_ref, wup_ref, wdn_ref, o_ref, *, d_ff, chunks, g_pre, g_post):
    x = x_ref[...]
    h = _rms(x, g_ref[g_pre:g_pre + 1, :]).astype(BF16)
    acc = None
    for c0, cw in chunks:
        gate = _dot(h, wup_ref[:, c0:c0 + cw])
        up = _dot(h, wup_ref[:, d_ff + c0:d_ff + c0 + cw])
        act = (gate * _sigmoid(gate) * up).astype(BF16)
        part = _dot(act, wdn_ref[c0:c0 + cw, :])
        acc = part if acc is None else acc + part
    o_ref[...] = x + 0.5 * _rms(acc, g_ref[g_post:g_post + 1, :])


def _ffn(x, norm_g, w_up, w_dn, layer, which, tm):
    rows, d = x.shape
    d_ff = w_dn.shape[2]
    chunk = 1024
    chunks = tuple((c0, min(chunk, d_ff - c0)) for c0 in range(0, d_ff, chunk))
    kern = functools.partial(_ffn_kernel, d_ff=d_ff, chunks=chunks,
                             g_pre=0 if which == 0 else 4, g_post=1 if which == 0 else 5)
    return pl.pallas_call(
        kern,
        grid=(rows // tm,),
        in_specs=[
            pl.BlockSpec((tm, d), lambda i: (i, 0)),
            _resident((None, 6, d), lambda i: (layer, 0, 0)),
            _resident((None, None, d, 2 * d_ff), lambda i: (layer, which, 0, 0)),
            _resident((None, None, d_ff, d), lambda i: (layer, which, 0, 0)),
        ],
        out_specs=pl.BlockSpec((tm, d), lambda i: (i, 0)),
        out_shape=jax.ShapeDtypeStruct((rows, d), F32),
        compiler_params=_cparams("parallel"),
        name=f"ffn{which}",
    )(x, norm_g, w_up, w_dn)


def _qkv_kernel(x_ref, g_ref, w_ref, q_ref, k_ref, v_ref, *, qw, kw, scale):
    h = _rms(x_ref[...], g_ref[2:3, :]).astype(BF16)
    z = _dot(h, w_ref[...])
    q_ref[...] = (z[:, :qw] * scale).astype(BF16)
    k_ref[...] = z[:, qw:qw + kw]
    v_ref[...] = z[:, qw + kw:]


def _zrest_kernel(x_ref, g_ref, w_ref, z_ref):
    h = _rms(x_ref[...], g_ref[2:3, :]).astype(BF16)
    z_ref[...] = _dot(h, w_ref[...])


def _inproj(x, norm_g, w_in, layer, tm, qw, kw, scale):
    rows, d = x.shape
    in_w = w_in.shape[2]
    cw = qw + 2 * kw
    n_rest = (in_w - cw) // cw
    assert cw * (n_rest + 1) == in_w
    q, k, v = pl.pallas_call(
        functools.partial(_qkv_kernel, qw=qw, kw=kw, scale=scale),
        grid=(rows // tm,),
        in_specs=[
            pl.BlockSpec((tm, d), lambda i: (i, 0)),
            _resident((None, 6, d), lambda i: (layer, 0, 0)),
            _resident((None, d, cw), lambda i: (layer, 0, 0)),
        ],
        out_specs=[
            pl.BlockSpec((tm, qw), lambda i: (i, 0)),
            pl.BlockSpec((tm, kw), lambda i: (i, 0)),
            pl.BlockSpec((tm, kw), lambda i: (i, 0)),
        ],
        out_shape=[
            jax.ShapeDtypeStruct((rows, qw), BF16),
            jax.ShapeDtypeStruct((rows, kw), F32),
            jax.ShapeDtypeStruct((rows, kw), F32),
        ],
        compiler_params=_cparams("parallel"),
        name="inproj_qkv",
    )(x, norm_g, w_in)
    zr = pl.pallas_call(
        _zrest_kernel,
        grid=(n_rest, rows // tm),
        in_specs=[
            pl.BlockSpec((tm, d), lambda j, i: (i, 0)),
            _resident((None, 6, d), lambda j, i: (layer, 0, 0)),
            pl.BlockSpec((None, d, cw), lambda j, i: (layer, 0, j + 1)),
        ],
        out_specs=pl.BlockSpec((tm, cw), lambda j, i: (i, j)),
        out_shape=jax.ShapeDtypeStruct((rows, in_w - cw), F32),
        compiler_params=_cparams("parallel", "parallel"),
        name="inproj_rest",
    )(x, norm_g, w_in)
    return q, k, v, zr


def _lambda_full(lqk_ref, lam_init):
    s01 = jnp.sum(lqk_ref[0:1, :] * lqk_ref[1:2, :], axis=-1, keepdims=True)
    s23 = jnp.sum(lqk_ref[2:3, :] * lqk_ref[3:4, :], axis=-1, keepdims=True)
    return jnp.exp(s01) - jnp.exp(s23) + lam_init


def _subln(o, sg, lam_init):
    return _rms(o, sg) * (1.0 - lam_init)


def _loop_by_pairs(n, fn):
    def body(jj, carry):
        fn(2 * jj)
        fn(2 * jj + 1)
        return carry

    lax.fori_loop(0, n // 2, body, 0)

    @pl.when(n % 2 == 1)
    def _():
        fn(n - 1)


def _attn_kernel(slopes_ref, q_ref, k_ref, v_ref, lqk_ref, sg_ref, o_ref,
                 kaug, vbf, s_scr, mx_scr, ls_scr, acc_scr, *, tq, lam_init):
    kvh = pl.program_id(1)
    i = pl.program_id(2)
    seq, hd2 = k_ref.shape
    hd = hd2 // 2
    tk = tq
    nc = tk // 128
    lane = lax.broadcasted_iota(jnp.int32, (1, hd2), 1)

    @pl.when(i == 0)
    def _():
        pos = lax.broadcasted_iota(jnp.int32, (seq, 1), 0)
        lo = pos & 7
        hi = (pos - lo).astype(F32)
        lo = lo.astype(F32)
        k = k_ref[...]
        kaug[0] = jnp.where(lane < hd, k, jnp.where(lane == hd, hi,
                            jnp.where(lane == hd + 1, lo, 0.0))).astype(BF16)
        kaug[1] = jnp.where(lane >= hd, k, jnp.where(lane == 0, hi,
                            jnp.where(lane == 1, lo, 0.0))).astype(BF16)
        vbf[...] = v_ref[...].astype(BF16)

    q = q_ref[...].astype(F32)
    qs = jnp.concatenate([q[:, :hd2], q[:, hd2:]], axis=0)
    row = lax.broadcasted_iota(jnp.int32, (2 * tq, 1), 0)
    slope = jnp.where(row < tq, slopes_ref[2 * kvh], slopes_ref[2 * kvh + 1])
    qm = (jnp.where(lane < hd, qs, jnp.where(lane < hd + 2, slope, 0.0)).astype(BF16),
          jnp.where(lane >= hd, qs, jnp.where(lane < 2, slope, 0.0)).astype(BF16))
    kcol = lax.broadcasted_iota(jnp.int32, (1, tk), 1)
    qloc = jnp.where(row < tq, row, row - tq)

    mx_scr[...] = jnp.full(mx_scr.shape, MASK_VALUE, F32)

    def scores(j):
        k0 = pl.multiple_of(j * tk, tk)
        visible = kcol + (k0 - i * tq) <= qloc
        for mm in range(2):
            s = _dot_nt(qm[mm], kaug[mm, pl.ds(k0, tk), :])
            s = jnp.where(visible, s, MASK_VALUE)
            s_scr[mm, j] = s
            mx = mx_scr[mm]
            for c in range(nc):
                mx = jnp.maximum(mx, s[:, c * 128:(c + 1) * 128])
            mx_scr[mm] = mx

    _loop_by_pairs(i + 1, scores)

    for mm in range(2):
        mx_scr[mm] = jnp.broadcast_to(jnp.max(mx_scr[mm], axis=-1, keepdims=True), mx_scr.shape[1:])
    ls_scr[...] = jnp.zeros(ls_scr.shape, F32)
    acc_scr[...] = jnp.zeros(acc_scr.shape, F32)

    def weigh(j):
        k0 = pl.multiple_of(j * tk, tk)
        vb = vbf[pl.ds(k0, tk), :]
        for mm in range(2):
            s = s_scr[mm, j]
            mb = mx_scr[mm]
            ps = [jnp.exp(s[:, c * 128:(c + 1) * 128] - mb) for c in range(nc)]
            ls = ls_scr[mm]
            for c in range(nc):
                ls = ls + ps[c]
            ls_scr[mm] = ls
            p = jnp.concatenate(ps, axis=-1).astype(BF16)
            acc_scr[mm] = acc_scr[mm] + _dot(p, vb)

    _loop_by_pairs(i + 1, weigh)

    lam = _lambda_full(lqk_ref, lam_init)
    l0 = jnp.sum(ls_scr[0], axis=-1, keepdims=True)
    l1 = jnp.sum(ls_scr[1], axis=-1, keepdims=True)
    o = acc_scr[0] / l0 - lam * (acc_scr[1] / l1)
    o = _subln(o, sg_ref[...], lam_init).astype(o_ref.dtype)
    o_ref[:, :hd2] = o[:tq]
    o_ref[:, hd2:] = o[tq:]


def _attn_prompt(q, k, v, slopes, lambda_qk, subln_g, layer, batch, seq, tq, lam_init):
    rows, qw = q.shape
    hd2 = k.shape[1] // N_KV_HEADS
    nq = seq // tq
    assert tq % 128 == 0 and seq % 8 == 0
    kern = functools.partial(_attn_kernel, tq=tq, lam_init=lam_init)
    return pl.pallas_call(
        kern,
        grid_spec=pltpu.PrefetchScalarGridSpec(
            num_scalar_prefetch=1,
            grid=(batch, N_KV_HEADS, nq),
            in_specs=[
                pl.BlockSpec((tq, HEAD_GROUP * hd2), lambda b, h, i, s: (b * nq + i, h)),
                pl.BlockSpec((seq, hd2), lambda b, h, i, s: (b, h)),
                pl.BlockSpec((seq, hd2), lambda b, h, i, s: (b, h)),
                pl.BlockSpec((None, 4, hd2 // 2), lambda b, h, i, s: (layer, 0, 0)),
                pl.BlockSpec((None, 1, hd2), lambda b, h, i, s: (layer, 0, 0)),
            ],
            out_specs=pl.BlockSpec((tq, HEAD_GROUP * hd2), lambda b, h, i, s: (b * nq + i, h)),
            scratch_shapes=[
                pltpu.VMEM((2, seq, hd2), BF16),
                pltpu.VMEM((seq, hd2), BF16),
                pltpu.VMEM((2, nq, 2 * tq, tq), F32),
                pltpu.VMEM((2, 2 * tq, 128), F32),
                pltpu.VMEM((2, 2 * tq, 128), F32),
                pltpu.VMEM((2, 2 * tq, hd2), F32),
            ],
        ),
        out_shape=jax.ShapeDtypeStruct((rows, qw), BF16),
        compiler_params=_cparams("parallel", "parallel", "arbitrary"),
        name="attn_prompt",
    )(slopes, q, k, v, lambda_qk, subln_g)


def _paged_kernel(pt_ref, slopes_ref, qt_ref, ks_ref, vs_ref, lqk_ref, sg_ref, *refs,
                  pages_per_step, page, past, lam_init):
    del pt_ref
    pp = pages_per_step
    kt_refs = refs[:pp]
    v_refs = refs[pp:2 * pp]
    o_ref = refs[2 * pp]
    m_ref, l_ref, acc_ref = refs[2 * pp + 1:]
    c = pl.program_id(1)
    nrow = qt_ref.shape[0]
    hd2 = sg_ref.shape[1]
    qt = qt_ref[...]
    row = lax.broadcasted_iota(jnp.int32, (nrow, 1), 0)
    row_kv = row // (2 * HEAD_GROUP)
    head = row // 2
    slope = jnp.zeros((nrow, 1), F32)
    for h in range(N_HEADS):
        slope = jnp.where(head == h, slopes_ref[h], slope)

    @pl.when(c == 0)
    def _():
        ks = ks_ref[...].astype(BF16).astype(F32)
        m_ref[...] = jnp.sum(qt.astype(F32) * ks, axis=-1, keepdims=True)
        l_ref[...] = jnp.ones(l_ref.shape, F32)
        vs = vs_ref[...].astype(BF16).astype(F32)
        a = jnp.zeros(acc_ref.shape, F32)
        for kv in range(N_KV_HEADS):
            a = jnp.where(row_kv == kv, vs[:, kv * hd2:(kv + 1) * hd2], a)
        acc_ref[...] = a

    s_parts = []
    for r in range(pp):
        kpos = (c * pp + r) * page + lax.broadcasted_iota(jnp.int32, (1, page), 1)
        dist = (past - kpos).astype(F32)
        s_parts.append(_dot(qt, kt_refs[r][...].astype(BF16)) - slope * dist)
    s = jnp.concatenate(s_parts, axis=-1)
    m_old = m_ref[...]
    m_new = jnp.maximum(m_old, jnp.max(s, axis=-1, keepdims=True))
    p = jnp.exp(s - m_new)
    corr = jnp.exp(m_old - m_new)
    l_ref[...] = corr * l_ref[...] + jnp.sum(p, axis=-1, keepdims=True)
    pv = None
    for kv in range(N_KV_HEADS):
        pk = jnp.where(row_kv == kv, p, 0.0).astype(BF16)
        for r in range(pp):
            v_kv = v_refs[r][pl.ds(kv, page, stride=N_KV_HEADS), :].astype(BF16)
            part = _dot(pk[:, r * page:(r + 1) * page], v_kv)
            pv = part if pv is None else pv + part
    acc_ref[...] = corr * acc_ref[...] + pv
    m_ref[...] = m_new

    @pl.when(c == pl.num_programs(1) - 1)
    def _():
        lam = _lambda_full(lqk_ref, lam_init)
        acc_ref[...] = acc_ref[...] / l_ref[...]
        a0 = acc_ref[pl.ds(0, N_HEADS, stride=2), :]
        a1 = acc_ref[pl.ds(1, N_HEADS, stride=2), :]
        o_ref[...] = _subln(a0 - lam * a1, sg_ref[...], lam_init).astype(o_ref.dtype)


def _attn_paged(qt, k_new, v_new, cache_kt, cache_v2, page_table, slopes, lambda_qk, subln_g,
                layer, pages_per_step, lam_init):
    nb, nrow, kvw = qt.shape
    page = cache_kt.shape[3]
    n_pages = page_table.shape[1]
    pp = pages_per_step
    hd2 = kvw // N_KV_HEADS
    assert cache_kt.shape[2] == kvw and cache_v2.shape[2:] == (page * N_KV_HEADS, hd2)
    kern = functools.partial(_paged_kernel, pages_per_step=pp, page=page,
                             past=n_pages * page, lam_init=lam_init)

    def page_spec(arr, r):
        return pl.BlockSpec((None, None) + arr.shape[2:],
                            lambda b, c, pt, s: (layer, pt[b, c * pp + r], 0, 0))

    return pl.pallas_call(
        kern,
        grid_spec=pltpu.PrefetchScalarGridSpec(
            num_scalar_prefetch=2,
            grid=(nb, n_pages // pp),
            in_specs=[
                pl.BlockSpec((None, nrow, kvw), lambda b, c, pt, s: (b, 0, 0)),
                pl.BlockSpec((None, 1, kvw), lambda b, c, pt, s: (b, 0, 0)),
                pl.BlockSpec((None, 1, kvw), lambda b, c, pt, s: (b, 0, 0)),
                pl.BlockSpec((None, 4, hd2 // 2), lambda b, c, pt, s: (layer, 0, 0)),
                pl.BlockSpec((None, 1, hd2), lambda b, c, pt, s: (layer, 0, 0)),
            ] + [page_spec(cache_kt, r) for r in range(pp)]
              + [page_spec(cache_v2, r) for r in range(pp)],
            out_specs=pl.BlockSpec((None, N_HEADS, hd2), lambda b, c, pt, s: (b, 0, 0)),
            scratch_shapes=[
                pltpu.VMEM((nrow, 1), F32),
                pltpu.VMEM((nrow, 1), F32),
                pltpu.VMEM((nrow, hd2), F32),
            ],
        ),
        out_shape=jax.ShapeDtypeStruct((nb, N_HEADS, hd2), BF16),
        compiler_params=_cparams("parallel", "arbitrary"),
        name="attn_paged",
    )(page_table, slopes, qt, k_new, v_new, lambda_qk, subln_g,
      *([cache_kt] * pp), *([cache_v2] * pp))


def _lru_gates(xc, gw_ref, gb_ref, lam_ref):
    w = xc.shape[1]
    bw = w // LRU_BLOCKS
    g0, g1 = [], []
    for n in range(LRU_BLOCKS):
        xb = xc[:, n * bw:(n + 1) * bw].astype(BF16)
        g0.append(_dot(xb, gw_ref[0, n]))
        g1.append(_dot(xb, gw_ref[1, n]))
    r = _sigmoid(jnp.concatenate(g0, axis=-1) + gb_ref[0:1, :])
    i = _sigmoid(jnp.concatenate(g1, axis=-1) + gb_ref[1:2, :])
    neg_lam = -lam_ref[...]
    softplus = jnp.maximum(neg_lam, 0.0) + jnp.log1p(jnp.exp(-jnp.abs(neg_lam)))
    log_a = -LRU_C * r * softplus
    a = jnp.exp(log_a)
    u = jnp.sqrt(1.0 - a * a) * (i * xc)
    return a, u


def _scan_rows(a, u):
    rows = a.shape[0]
    ridx = lax.broadcasted_iota(jnp.int32, (rows, 1), 0)
    s = 1
    while s < rows:
        if s < 8:
            keep = ridx >= s
            a_sh = jnp.where(keep, pltpu.roll(a, s, 0), 1.0)
            u_sh = jnp.where(keep, pltpu.roll(u, s, 0), 0.0)
        else:
            a_sh = jnp.concatenate([jnp.ones((s, a.shape[1]), F32), a[:rows - s]], axis=0)
            u_sh = jnp.concatenate([jnp.zeros((s, a.shape[1]), F32), u[:rows - s]], axis=0)
        u = a * u_sh + u
        a = a * a_sh
        s *= 2
    return a, u


def _mixer_kernel(rgx_ref, rgg_ref, scb_ref, scc_ref, scx_ref, h0_ref, rgbuf_ref, scbuf_ref,
                  cw_ref, cb_ref, gw_ref, gb_ref, lam_ref, scw_ref,
                  yrg_ref, ysc_ref, hout_ref, rgbuf_out_ref, scbuf_out_ref,
                  xp_rg, xp_sc, hcarry, *, tt):
    t = pl.program_id(1)
    nrg = rgbuf_ref.shape[0]
    nsc = scbuf_ref.shape[0]

    @pl.when(t == 0)
    def _():
        xp_rg[8 - nrg:8, :] = rgbuf_ref[...]
        xp_sc[8 - nsc:8, :] = scbuf_ref[...]
        hcarry[...] = h0_ref[...]

    xp_rg[8:8 + tt, :] = rgx_ref[...]
    xc = cb_ref[...] + cw_ref[0:1, :] * xp_rg[8 - nrg:8 - nrg + tt, :]
    for j in range(1, nrg + 1):
        xc = xc + cw_ref[j:j + 1, :] * xp_rg[8 - nrg + j:8 - nrg + j + tt, :]
    a, u = _lru_gates(xc, gw_ref, gb_ref, lam_ref)
    a_cum, h_loc = _scan_rows(a, u)
    h = h_loc + a_cum * hcarry[...]
    hcarry[...] = h[tt - 1:tt, :]
    yrg_ref[...] = (h * _gelu_tanh(rgg_ref[...])).astype(yrg_ref.dtype)
    xp_rg[8 - nrg:8, :] = xp_rg[8 + tt - nrg:8 + tt, :]

    xp_sc[8:8 + tt, :] = scc_ref[...] * scx_ref[...]
    y = scw_ref[0:1, :] * xp_sc[8 - nsc:8 - nsc + tt, :]
    for j in range(1, nsc + 1):
        y = y + scw_ref[j:j + 1, :] * xp_sc[8 - nsc + j:8 - nsc + j + tt, :]
    ysc_ref[...] = (scb_ref[...] * y).astype(ysc_ref.dtype)
    xp_sc[8 - nsc:8, :] = xp_sc[8 + tt - nsc:8 + tt, :]

    @pl.when(t == pl.num_programs(1) - 1)
    def _():
        hout_ref[...] = hcarry[...]
        rgbuf_out_ref[...] = xp_rg[8 - nrg:8, :]
        scbuf_out_ref[...] = xp_sc[8 - nsc:8, :]


def _mixer_prompt(zr, h0, rg_buf, sc_buf, rg_conv_w, rg_conv_b, gate_w, rg_gate_b, rg_lambda,
                  sc_conv_w, layer, batch, seq, tt):
    rows = zr.shape[0]
    w = h0.shape[-1]
    nt = seq // tt
    nrg, nsc = rg_buf.shape[1], sc_buf.shape[1]

    def zcol(cidx):
        return pl.BlockSpec((tt, w), lambda b, t: (b * nt + t, cidx))

    def per_batch(n):
        return pl.BlockSpec((None, n, w), lambda b, t: (b, 0, 0))

    def per_layer(n):
        return pl.BlockSpec((None, n, w), lambda b, t: (layer, 0, 0))

    bw = w // LRU_BLOCKS
    return pl.pallas_call(
        functools.partial(_mixer_kernel, tt=tt),
        grid=(batch, nt),
        in_specs=[zcol(0), zcol(1), zcol(2), zcol(3), zcol(4),
                  per_batch(1), per_batch(nrg), per_batch(nsc),
                  per_layer(nrg + 1), per_layer(1),
                  pl.BlockSpec((None, 2, LRU_BLOCKS, bw, bw), lambda b, t: (layer, 0, 0, 0, 0)),
                  per_layer(2), per_layer(1), per_layer(nsc + 1)],
        out_specs=[pl.BlockSpec((tt, w), lambda b, t: (b * nt + t, 0)),
                   pl.BlockSpec((tt, w), lambda b, t: (b * nt + t, 0)),
                   per_batch(1), per_batch(nrg), per_batch(nsc)],
        out_shape=[jax.ShapeDtypeStruct((rows, w), BF16),
                   jax.ShapeDtypeStruct((rows, w), BF16),
                   jax.ShapeDtypeStruct((batch, 1, w), F32),
                   jax.ShapeDtypeStruct((batch, nrg, w), F32),
                   jax.ShapeDtypeStruct((batch, nsc, w), F32)],
        scratch_shapes=[pltpu.VMEM((8 + tt, w), F32), pltpu.VMEM((8 + tt, w), F32),
                        pltpu.VMEM((1, w), F32)],
        compiler_params=_cparams("parallel", "arbitrary"),
        name="mixer_prompt",
    )(zr, zr, zr, zr, zr, h0, rg_buf, sc_buf, rg_conv_w, rg_conv_b, gate_w, rg_gate_b,
      rg_lambda, sc_conv_w)


def _mixer_step_kernel(rgx_ref, rgg_ref, scb_ref, scc_ref, scx_ref, h0_ref, rgbuf_ref, scbuf_ref,
                       cw_ref, cb_ref, gw_ref, gb_ref, lam_ref, scw_ref,
                       yrg_ref, ysc_ref, hout_ref, rgbuf_out_ref, scbuf_out_ref, *, nrg, nsc):
    w = h0_ref.shape[1]
    x = rgx_ref[...]
    xc = cb_ref[...] + cw_ref[nrg:nrg + 1, :] * x
    for j in range(nrg):
        xc = xc + cw_ref[j:j + 1, :] * rgbuf_ref[:, j * w:(j + 1) * w]
    a, u = _lru_gates(xc, gw_ref, gb_ref, lam_ref)
    h = a * h0_ref[...] + u
    hout_ref[...] = h
    yrg_ref[...] = (h * _gelu_tanh(rgg_ref[...])).astype(yrg_ref.dtype)
    for j in range(nrg - 1):
        rgbuf_out_ref[:, j * w:(j + 1) * w] = rgbuf_ref[:, (j + 1) * w:(j + 2) * w]
    rgbuf_out_ref[:, (nrg - 1) * w:] = x

    cx = scc_ref[...] * scx_ref[...]
    y = scw_ref[nsc:nsc + 1, :] * cx
    for j in range(nsc):
        y = y + scw_ref[j:j + 1, :] * scbuf_ref[:, j * w:(j + 1) * w]
    ysc_ref[...] = (scb_ref[...] * y).astype(ysc_ref.dtype)
    for j in range(nsc - 1):
        scbuf_out_ref[:, j * w:(j + 1) * w] = scbuf_ref[:, (j + 1) * w:(j + 2) * w]
    scbuf_out_ref[:, (nsc - 1) * w:] = cx


def _mixer_sample(zr, h0, rg_buf, sc_buf, rg_conv_w, rg_conv_b, gate_w, rg_gate_b, rg_lambda,
                  sc_conv_w, layer):
    nb, w = h0.shape
    nrg, nsc = rg_buf.shape[1], sc_buf.shape[1]
    bw = w // LRU_BLOCKS

    def zcol(cidx):
        return pl.BlockSpec((nb, w), lambda i: (0, cidx))

    def full(n):
        return pl.BlockSpec((nb, n * w), lambda i: (0, 0))

    def per_layer(n):
        return pl.BlockSpec((None, n, w), lambda i: (layer, 0, 0))

    outs = pl.pallas_call(
        functools.partial(_mixer_step_kernel, nrg=nrg, nsc=nsc),
        grid=(1,),
        in_specs=[zcol(0), zcol(1), zcol(2), zcol(3), zcol(4),
                  full(1), full(nrg), full(nsc),
                  per_layer(nrg + 1), per_layer(1),
                  pl.BlockSpec((None, 2, LRU_BLOCKS, bw, bw), lambda i: (layer, 0, 0, 0, 0)),
                  per_layer(2), per_layer(1), per_layer(nsc + 1)],
        out_specs=[full(1), full(1), full(1), full(nrg), full(nsc)],
        out_shape=[jax.ShapeDtypeStruct((nb, w), BF16),
                   jax.ShapeDtypeStruct((nb, w), BF16),
                   jax.ShapeDtypeStruct((nb, w), F32),
                   jax.ShapeDtypeStruct((nb, nrg * w), F32),
                   jax.ShapeDtypeStruct((nb, nsc * w), F32)],
        compiler_params=_cparams("arbitrary"),
        name="mixer_sample",
    )(zr, zr, zr, zr, zr, h0, rg_buf.reshape(nb, nrg * w), sc_buf.reshape(nb, nsc * w),
      rg_conv_w, rg_conv_b, gate_w, rg_gate_b, rg_lambda, sc_conv_w)
    y_rg, y_sc, h, rgb, scb = outs
    return y_rg, y_sc, h, rgb.reshape(nb, nrg, w), scb.reshape(nb, nsc, w)


def _merge_kernel(x_ref, o_ref, yrg_ref, ysc_ref, gz0_ref, gz1_ref, gz2_ref, g_ref,
                  wa_ref, wr_ref, ws_ref, wo_ref, out_ref):
    m = _sigmoid(gz0_ref[...]) * _dot(o_ref[...], wa_ref[...])
    m = m + _sigmoid(gz1_ref[...]) * _dot(yrg_ref[...], wr_ref[...])
    m = m + _sigmoid(gz2_ref[...]) * _dot(ysc_ref[...], ws_ref[...])
    y = _dot(m.astype(BF16), wo_ref[...])
    out_ref[...] = x_ref[...] + _rms(y, g_ref[3:4, :])


def _merge(x, o, y_rg, y_sc, zr, norm_g, w_a, w_r, w_s, w_o, layer, tm, gz_col):
    rows, d = x.shape

    def rowblk(cidx=0):
        return pl.BlockSpec((tm, d), lambda i: (i, cidx))

    def wspec(arr):
        return _resident((None,) + arr.shape[1:], lambda i: (layer, 0, 0))

    return pl.pallas_call(
        _merge_kernel,
        grid=(rows // tm,),
        in_specs=[rowblk(), rowblk(), rowblk(), rowblk(),
                  rowblk(gz_col), rowblk(gz_col + 1), rowblk(gz_col + 2),
                  _resident((None, 6, d), lambda i: (layer, 0, 0)),
                  wspec(w_a), wspec(w_r), wspec(w_s), wspec(w_o)],
        out_specs=rowblk(),
        out_shape=jax.ShapeDtypeStruct((rows, d), F32),
        compiler_params=_cparams("parallel"),
        name="merge",
    )(x, o, y_rg, y_sc, zr, zr, zr, norm_g, w_a, w_r, w_s, w_o)


def _pick_tile(n, pref):
    t = min(n, pref)
    while n % t:
        t //= 2
    return t


def kernel(x_prompt, x_sample, cache_k, cache_v, page_table, state_rglru_h, state_rglru_conv, state_sconv, norm_g, w_ffn_up, w_ffn_down, w_in, lambda_qk, subln_g, rg_conv_w, rg_conv_b, rg_gate_w, rg_gate_b, rg_lambda, sc_conv_w, w_branch_attn, w_branch_rg, w_branch_sc, w_out):
    batch, seq, d = x_prompt.shape
    nb, dec_seq, _ = x_sample.shape
    assert dec_seq == 1, "the sample group carries one new token per sequence"
    depth = w_in.shape[0]
    n_pool, page = cache_k.shape[1], cache_k.shape[2]
    head_dim = cache_k.shape[-1]
    hd2 = 2 * head_dim
    qw = N_HEADS * hd2
    kw = N_KV_HEADS * hd2
    w = state_rglru_h.shape[-1]
    assert qw == d and w == d and qw + 2 * kw == 2 * d
    scale = head_dim ** -0.5
    assert math.log2(scale) == round(math.log2(scale)), "q pre-scaling must be exact in bf16"
    n_pages = page_table.shape[1]
    nrg, nsc = state_rglru_conv.shape[2], state_sconv.shape[2]

    w_up_b = w_ffn_up.astype(BF16)
    w_dn_b = w_ffn_down.astype(BF16)
    w_in_b = w_in.astype(BF16)
    w_a_b = w_branch_attn.astype(BF16)
    w_r_b = w_branch_rg.astype(BF16)
    w_s_b = w_branch_sc.astype(BF16)
    w_o_b = w_out.astype(BF16)
    gate_w_b = rg_gate_w.astype(BF16)

    slopes_np = np.float32(2.0) ** (-8.0 * np.arange(1, N_HEADS + 1, dtype=np.float32) / N_HEADS)
    assert all(math.frexp(float(s))[0] == 0.5 for s in slopes_np), "slopes must be bf16-exact"
    slopes = jnp.asarray(slopes_np, F32)
    cache_kt = jnp.transpose(cache_k, (0, 1, 3, 4, 5, 2)).reshape(depth, n_pool, kw, page)
    cache_v2 = cache_v.reshape(depth, n_pool, page * N_KV_HEADS, hd2)
    rg_conv_b3 = rg_conv_b.reshape(depth, 1, w)
    rg_lambda3 = rg_lambda.reshape(depth, 1, w)
    subln_g3 = subln_g.reshape(depth, 1, hd2)
    eye_kv = jnp.eye(N_KV_HEADS, dtype=BF16)
    eye_m = jnp.eye(2, dtype=BF16)

    tm = _pick_tile(batch * seq, 512)
    tq = _pick_tile(seq, 256)
    tt = _pick_tile(seq, 256)
    pps = _pick_tile(n_pages, 8)
    gz_col = 5

    xp = x_prompt.reshape(batch * seq, d)
    xs = x_sample.reshape(nb, d)
    zeros_h = jnp.zeros((batch, 1, w), F32)
    zeros_rg = jnp.zeros((batch, nrg, w), F32)
    zeros_sc = jnp.zeros((batch, nsc, w), F32)

    p_states, s_states = [], []
    for l in range(depth):
        lam_init = 0.8 - 0.6 * math.exp(-0.3 * l)
        dense = dict(norm_g=norm_g, layer=l)

        xp = _ffn(xp, norm_g, w_up_b, w_dn_b, l, 0, tm)
        q, k, v, zr = _inproj(xp, norm_g, w_in_b, l, tm, qw, kw, scale)
        o = _attn_prompt(q, k, v, slopes, lambda_qk, subln_g3, l, batch, seq, tq, lam_init)
        y_rg, y_sc, h_p, rgb_p, scb_p = _mixer_prompt(
            zr, zeros_h, zeros_rg, zeros_sc, rg_conv_w, rg_conv_b3, gate_w_b, rg_gate_b,
            rg_lambda3, sc_conv_w, l, batch, seq, tt)
        xp = _merge(xp, o, y_rg, y_sc, zr, norm_g, w_a_b, w_r_b, w_s_b, w_o_b, l, tm, gz_col)
        xp = _ffn(xp, norm_g, w_up_b, w_dn_b, l, 1, tm)
        p_states.append((k.reshape(batch, seq, N_KV_HEADS, 2, head_dim),
                         v.reshape(batch, seq, N_KV_HEADS, hd2),
                         h_p.reshape(batch, w), rgb_p, scb_p))

        xs = _ffn(xs, norm_g, w_up_b, w_dn_b, l, 0, nb)
        q, k, v, zr = _inproj(xs, norm_g, w_in_b, l, nb, qw, kw, scale)
        q5 = q.reshape(nb, N_KV_HEADS, HEAD_GROUP, 2, head_dim)
        qt = jnp.einsum('bkgmd,kK,mM->bkgmKMd', q5, eye_kv, eye_m).reshape(nb, 2 * N_HEADS, kw)
        o = _attn_paged(qt, k.reshape(nb, 1, kw), v.reshape(nb, 1, kw), cache_kt, cache_v2,
                        page_table, slopes, lambda_qk, subln_g3, l, pps, lam_init)
        o = o.reshape(nb, qw)
        y_rg, y_sc, h_s, rgb_s, scb_s = _mixer_sample(
            zr, state_rglru_h[l], state_rglru_conv[l], state_sconv[l], rg_conv_w, rg_conv_b3,
            gate_w_b, rg_gate_b, rg_lambda3, sc_conv_w, l)
        xs = _merge(xs, o, y_rg, y_sc, zr, norm_g, w_a_b, w_r_b, w_s_b, w_o_b, l, nb, gz_col)
        xs = _ffn(xs, norm_g, w_up_b, w_dn_b, l, 1, nb)
        s_states.append((k.reshape(nb, 1, N_KV_HEADS, 2, head_dim),
                         v.reshape(nb, 1, N_KV_HEADS, hd2), h_s, rgb_s, scb_s))

    k_p, v_p, h_p, rgc_p, sc_p = [jnp.stack(s, axis=0) for s in zip(*p_states)]
    k_s, v_s, h_s, rgc_s, sc_s = [jnp.stack(s, axis=0) for s in zip(*s_states)]
    return (xp.reshape(batch, seq, d), xs.reshape(nb, 1, d), k_p, v_p, h_p, rgc_p, sc_p,
            k_s, v_s, h_s, rgc_s, sc_s)
```

```python
import functools
import math

import numpy as np
import jax
import jax.numpy as jnp
from jax import lax
from jax.experimental import pallas as pl
from jax.experimental.pallas import tpu as pltpu

F32 = jnp.float32
BF16 = jnp.bfloat16

NORM_EPS = 1e-6
LRU_C = 8.0
N_HEADS = 8
N_KV_HEADS = 4
HEAD_GROUP = N_HEADS // N_KV_HEADS
LRU_BLOCKS = 8
N_BRANCHES = 3
MASK_VALUE = -1e30
LOG2_E = math.log2(math.e)
VMEM_LIMIT_BYTES = 56 * 1024 * 1024


def _cparams(*sem):
    return pltpu.CompilerParams(dimension_semantics=sem, vmem_limit_bytes=VMEM_LIMIT_BYTES)


def _rms(x, g):
    return x * lax.rsqrt(jnp.mean(x * x, axis=-1, keepdims=True) + NORM_EPS) * g


def _dot(a, b):
    return jnp.dot(a, b, preferred_element_type=F32)


def _dot_nt(a, b):
    return lax.dot_general(a, b, (((1,), (1,)), ((), ())), preferred_element_type=F32)


def _sigmoid(x):
    return 0.5 * jnp.tanh(0.5 * x) + 0.5


def _gelu_tanh(x):
    c = math.sqrt(2.0 / math.pi)
    return 0.5 * x * (1.0 + jnp.tanh(c * (x + 0.044715 * (x * x * x))))


def _resident(shape, index_map):
    return pl.BlockSpec(shape, index_map, pipeline_mode=pl.Buffered(1))


def _ffn_kernel(x_ref, g_ref, wup_ref, wdn_ref, o_ref, *, d_ff, chunks, g_pre, g_post):
    x = x_ref[...]
    h = _rms(x, g_ref[g_pre:g_pre + 1, :]).astype(BF16)
    acc = None
    for c0, cw in chunks:
        gate = _dot(h, wup_ref[:, c0:c0 + cw])
        up = _dot(h, wup_ref[:, d_ff + c0:d_ff + c0 + cw])
        act = (gate * _sigmoid(gate) * up).astype(BF16)
        part = _dot(act, wdn_ref[c0:c0 + cw, :])
        acc = part if acc is None else acc + part
    o_ref[...] = x + 0.5 * _rms(acc, g_ref[g_post:g_post + 1, :])


def _ffn(x, norm_g, w_up, w_dn, layer, which, tm):
    rows, d = x.shape
    d_ff = w_dn.shape[2]
    chunk = 1024
    chunks = tuple((c0, min(chunk, d_ff - c0)) for c0 in range(0, d_ff, chunk))
    kern = functools.partial(_ffn_kernel, d_ff=d_ff, chunks=chunks,
                             g_pre=0 if which == 0 else 4, g_post=1 if which == 0 else 5)
    return pl.pallas_call(
        kern,
        grid=(rows // tm,),
        in_specs=[
            pl.BlockSpec((tm, d), lambda i: (i, 0)),
            _resident((None, 6, d), lambda i: (layer, 0, 0)),
            _resident((None, None, d, 2 * d_ff), lambda i: (layer, which, 0, 0)),
            _resident((None, None, d_ff, d), lambda i: (layer, which, 0, 0)),
        ],
        out_specs=pl.BlockSpec((tm, d), lambda i: (i, 0)),
        out_shape=jax.ShapeDtypeStruct((rows, d), F32),
        compiler_params=_cparams("parallel"),
        name=f"ffn{which}",
    )(x, norm_g, w_up, w_dn)


def _qkv_kernel(x_ref, g_ref, w_ref, q_ref, k_ref, v_ref, *, qw, kw, scale):
    h = _rms(x_ref[...], g_ref[2:3, :]).astype(BF16)
    z = _dot(h, w_ref[...])
    q_ref[...] = (z[:, :qw] * scale).astype(BF16)
    k_ref[...] = z[:, qw:qw + kw]
    v_ref[...] = z[:, qw + kw:]


def _zrest_kernel(x_ref, g_ref, w_ref, z_ref):
    h = _rms(x_ref[...], g_ref[2:3, :]).astype(BF16)
    z_ref[...] = _dot(h, w_ref[...])


def _inproj(x, norm_g, w_in, layer, tm, qw, kw, scale):
    rows, d = x.shape
    in_w = w_in.shape[2]
    cw = qw + 2 * kw
    n_rest = (in_w - cw) // cw
    assert cw * (n_rest + 1) == in_w
    q, k, v = pl.pallas_call(
        functools.partial(_qkv_kernel, qw=qw, kw=kw, scale=scale),
        grid=(rows // tm,),
        in_specs=[
            pl.BlockSpec((tm, d), lambda i: (i, 0)),
            _resident((None, 6, d), lambda i: (layer, 0, 0)),
            _resident((None, d, cw), lambda i: (layer, 0, 0)),
        ],
        out_specs=[
            pl.BlockSpec((tm, qw), lambda i: (i, 0)),
            pl.BlockSpec((tm, kw), lambda i: (i, 0)),
            pl.BlockSpec((tm, kw), lambda i: (i, 0)),
        ],
        out_shape=[
            jax.ShapeDtypeStruct((rows, qw), BF16),
            jax.ShapeDtypeStruct((rows, kw), F32),
            jax.ShapeDtypeStruct((rows, kw), F32),
        ],
        compiler_params=_cparams("parallel"),
        name="inproj_qkv",
    )(x, norm_g, w_in)
    zr = pl.pallas_call(
        _zrest_kernel,
        grid=(n_rest, rows // tm),
        in_specs=[
            pl.BlockSpec((tm, d), lambda j, i: (i, 0)),
            _resident((None, 6, d), lambda j, i: (layer, 0, 0)),
            pl.BlockSpec((None, d, cw), lambda j, i: (layer, 0, j + 1)),
        ],
        out_specs=pl.BlockSpec((tm, cw), lambda j, i: (i, j)),
        out_shape=jax.ShapeDtypeStruct((rows, in_w - cw), F32),
        compiler_params=_cparams("parallel", "parallel"),
        name="inproj_rest",
    )(x, norm_g, w_in)
    return q, k, v, zr


def _lambda_full(lqk_ref, lam_init):
    s01 = jnp.sum(lqk_ref[0:1, :] * lqk_ref[1:2, :], axis=-1, keepdims=True)
    s23 = jnp.sum(lqk_ref[2:3, :] * lqk_ref[3:4, :], axis=-1, keepdims=True)
    return jnp.exp(s01) - jnp.exp(s23) + lam_init


def _subln(o, sg, lam_init):
    return _rms(o, sg) * (1.0 - lam_init)


def _loop_by_fours(n, fn):
    def body(jj, carry):
        for u in range(4):
            fn(4 * jj + u)
        return carry

    lax.fori_loop(0, n // 4, body, 0)
    base = (n // 4) * 4

    @pl.when(n % 4 >= 2)
    def _():
        fn(base)
        fn(base + 1)

    @pl.when(n % 2 == 1)
    def _():
        fn(n - 1)


def _attn_kernel(slopes_ref, q_ref, k_ref, v_ref, lqk_ref, sg_ref, o_ref,
                 kaug, vbf, s_scr, mx_scr, ls_scr, acc_scr, *, tq, lam_init):
    kvh = pl.program_id(1)
    i = pl.program_id(2)
    seq, hd2 = k_ref.shape
    hd = hd2 // 2
    tk = tq
    nc = tk // 128
    lane = lax.broadcasted_iota(jnp.int32, (1, hd2), 1)

    @pl.when(i == 0)
    def _():
        pos = lax.broadcasted_iota(jnp.int32, (seq, 1), 0)
        lo = pos & 7
        hi = (pos - lo).astype(F32)
        lo = lo.astype(F32)
        k = k_ref[...]
        kaug[0] = jnp.where(lane < hd, k, jnp.where(lane == hd, hi,
                            jnp.where(lane == hd + 1, lo, 0.0))).astype(BF16)
        kaug[1] = jnp.where(lane >= hd, k, jnp.where(lane == 0, hi,
                            jnp.where(lane == 1, lo, 0.0))).astype(BF16)
        vbf[...] = v_ref[...].astype(BF16)

    q = q_ref[...].astype(F32)
    qs = jnp.concatenate([q[:, :hd2], q[:, hd2:]], axis=0)
    row = lax.broadcasted_iota(jnp.int32, (2 * tq, 1), 0)
    slope = jnp.where(row < tq, slopes_ref[2 * kvh], slopes_ref[2 * kvh + 1])
    qm = (jnp.where(lane < hd, qs, jnp.where(lane < hd + 2, slope, 0.0)).astype(BF16),
          jnp.where(lane >= hd, qs, jnp.where(lane < 2, slope, 0.0)).astype(BF16))
    kcol = lax.broadcasted_iota(jnp.int32, (1, tk), 1)
    qloc = jnp.where(row < tq, row, row - tq)

    mx_scr[...] = jnp.full(mx_scr.shape, MASK_VALUE, F32)

    def scores(j):
        k0 = pl.multiple_of(j * tk, tk)
        visible = kcol + (k0 - i * tq) <= qloc
        for mm in range(2):
            s = _dot_nt(qm[mm], kaug[mm, pl.ds(k0, tk), :])
            s = jnp.where(visible, s * LOG2_E, MASK_VALUE)
            s_scr[mm, j] = s
            mx = mx_scr[mm]
            for c in range(nc):
                mx = jnp.maximum(mx, s[:, c * 128:(c + 1) * 128])
            mx_scr[mm] = mx

    _loop_by_fours(i + 1, scores)

    for mm in range(2):
        mx_scr[mm] = jnp.broadcast_to(jnp.max(mx_scr[mm], axis=-1, keepdims=True), mx_scr.shape[1:])
    ls_scr[...] = jnp.zeros(ls_scr.shape, F32)
    acc_scr[...] = jnp.zeros(acc_scr.shape, F32)

    def weigh(j):
        k0 = pl.multiple_of(j * tk, tk)
        vb = vbf[pl.ds(k0, tk), :]
        for mm in range(2):
            s = s_scr[mm, j]
            mb = mx_scr[mm]
            ps = [jnp.exp2(s[:, c * 128:(c + 1) * 128] - mb) for c in range(nc)]
            ls = ls_scr[mm]
            for c in range(nc):
                ls = ls + ps[c]
            ls_scr[mm] = ls
            p = jnp.concatenate(ps, axis=-1).astype(BF16)
            acc_scr[mm] = acc_scr[mm] + _dot(p, vb)

    _loop_by_fours(i + 1, weigh)

    lam = _lambda_full(lqk_ref, lam_init)
    l0 = jnp.sum(ls_scr[0], axis=-1, keepdims=True)
    l1 = jnp.sum(ls_scr[1], axis=-1, keepdims=True)
    o = acc_scr[0] / l0 - lam * (acc_scr[1] / l1)
    o = _subln(o, sg_ref[...], lam_init).astype(o_ref.dtype)
    o_ref[:, :hd2] = o[:tq]
    o_ref[:, hd2:] = o[tq:]


def _attn_prompt(q, k, v, slopes, lambda_qk, subln_g, layer, batch, seq, tq, lam_init):
    rows, qw = q.shape
    hd2 = k.shape[1] // N_KV_HEADS
    nq = seq // tq
    assert tq % 128 == 0 and seq % 8 == 0
    kern = functools.partial(_attn_kernel, tq=tq, lam_init=lam_init)
    return pl.pallas_call(
        kern,
        grid_spec=pltpu.PrefetchScalarGridSpec(
            num_scalar_prefetch=1,
            grid=(batch, N_KV_HEADS, nq),
            in_specs=[
                pl.BlockSpec((tq, HEAD_GROUP * hd2), lambda b, h, i, s: (b * nq + i, h)),
                pl.BlockSpec((seq, hd2), lambda b, h, i, s: (b, h)),
                pl.BlockSpec((seq, hd2), lambda b, h, i, s: (b, h)),
                pl.BlockSpec((None, 4, hd2 // 2), lambda b, h, i, s: (layer, 0, 0)),
                pl.BlockSpec((None, 1, hd2), lambda b, h, i, s: (layer, 0, 0)),
            ],
            out_specs=pl.BlockSpec((tq, HEAD_GROUP * hd2), lambda b, h, i, s: (b * nq + i, h)),
            scratch_shapes=[
                pltpu.VMEM((2, seq, hd2), BF16),
                pltpu.VMEM((seq, hd2), BF16),
                pltpu.VMEM((2, nq, 2 * tq, tq), F32),
                pltpu.VMEM((2, 2 * tq, 128), F32),
                pltpu.VMEM((2, 2 * tq, 128), F32),
                pltpu.VMEM((2, 2 * tq, hd2), F32),
            ],
        ),
        out_shape=jax.ShapeDtypeStruct((rows, qw), BF16),
        compiler_params=_cparams("parallel", "parallel", "arbitrary"),
        name="attn_prompt",
    )(slopes, q, k, v, lambda_qk, subln_g)


def _paged_kernel(pt_ref, slopes_ref, qt_ref, ks_ref, vs_ref, lqk_ref, sg_ref, *refs,
                  pages_per_step, page, past, lam_init):
    del pt_ref
    pp = pages_per_step
    kt_refs = refs[:pp]
    v_refs = refs[pp:2 * pp]
    o_ref = refs[2 * pp]
    m_ref, l_ref, acc_ref = refs[2 * pp + 1:]
    c = pl.program_id(1)
    nrow = qt_ref.shape[0]
    hd2 = sg_ref.shape[1]
    qt = qt_ref[...]
    row = lax.broadcasted_iota(jnp.int32, (nrow, 1), 0)
    row_kv = row // (2 * HEAD_GROUP)
    head = row // 2
    slope = jnp.zeros((nrow, 1), F32)
    for h in range(N_HEADS):
        slope = jnp.where(head == h, slopes_ref[h], slope)

    @pl.when(c == 0)
    def _():
        ks = ks_ref[...].astype(BF16).astype(F32)
        m_ref[...] = jnp.sum(qt.astype(F32) * ks, axis=-1, keepdims=True)
        l_ref[...] = jnp.ones(l_ref.shape, F32)
        vs = vs_ref[...].astype(BF16).astype(F32)
        a = jnp.zeros(acc_ref.shape, F32)
        for kv in range(N_KV_HEADS):
            a = jnp.where(row_kv == kv, vs[:, kv * hd2:(kv + 1) * hd2], a)
        acc_ref[...] = a

    s_parts = []
    for r in range(pp):
        kpos = (c * pp + r) * page + lax.broadcasted_iota(jnp.int32, (1, page), 1)
        dist = (past - kpos).astype(F32)
        s_parts.append(_dot(qt, kt_refs[r][...].astype(BF16)) - slope * dist)
    s = jnp.concatenate(s_parts, axis=-1)
    m_old = m_ref[...]
    m_new = jnp.maximum(m_old, jnp.max(s, axis=-1, keepdims=True))
    p = jnp.exp(s - m_new)
    corr = jnp.exp(m_old - m_new)
    l_ref[...] = corr * l_ref[...] + jnp.sum(p, axis=-1, keepdims=True)
    pv = None
    for kv in range(N_KV_HEADS):
        pk = jnp.where(row_kv == kv, p, 0.0).astype(BF16)
        for r in range(pp):
            v_kv = v_refs[r][pl.ds(kv, page, stride=N_KV_HEADS), :].astype(BF16)
            part = _dot(pk[:, r * page:(r + 1) * page], v_kv)
            pv = part if pv is None else pv + part
    acc_ref[...] = corr * acc_ref[...] + pv
    m_ref[...] = m_new

    @pl.when(c == pl.num_programs(1) - 1)
    def _():
        lam = _lambda_full(lqk_ref, lam_init)
        acc_ref[...] = acc_ref[...] / l_ref[...]
        a0 = acc_ref[pl.ds(0, N_HEADS, stride=2), :]
        a1 = acc_ref[pl.ds(1, N_HEADS, stride=2), :]
        o_ref[...] = _subln(a0 - lam * a1, sg_ref[...], lam_init).astype(o_ref.dtype)


def _attn_paged(qt, k_new, v_new, cache_kt, cache_v2, page_table, slopes, lambda_qk, subln_g,
                layer, pages_per_step, lam_init):
    nb, nrow, kvw = qt.shape
    page = cache_kt.shape[3]
    n_pages = page_table.shape[1]
    pp = pages_per_step
    hd2 = kvw // N_KV_HEADS
    assert cache_kt.shape[2] == kvw and cache_v2.shape[2:] == (page * N_KV_HEADS, hd2)
    kern = functools.partial(_paged_kernel, pages_per_step=pp, page=page,
                             past=n_pages * page, lam_init=lam_init)

    def page_spec(arr, r):
        return pl.BlockSpec((None, None) + arr.shape[2:],
                            lambda b, c, pt, s: (layer, pt[b, c * pp + r], 0, 0))

    return pl.pallas_call(
        kern,
        grid_spec=pltpu.PrefetchScalarGridSpec(
            num_scalar_prefetch=2,
            grid=(nb, n_pages // pp),
            in_specs=[
                pl.BlockSpec((None, nrow, kvw), lambda b, c, pt, s: (b, 0, 0)),
                pl.BlockSpec((None, 1, kvw), lambda b, c, pt, s: (b, 0, 0)),
                pl.BlockSpec((None, 1, kvw), lambda b, c, pt, s: (b, 0, 0)),
                pl.BlockSpec((None, 4, hd2 // 2), lambda b, c, pt, s: (layer, 0, 0)),
                pl.BlockSpec((None, 1, hd2), lambda b, c, pt, s: (layer, 0, 0)),
            ] + [page_spec(cache_kt, r) for r in range(pp)]
              + [page_spec(cache_v2, r) for r in range(pp)],
            out_specs=pl.BlockSpec((None, N_HEADS, hd2), lambda b, c, pt, s: (b, 0, 0)),
            scratch_shapes=[
                pltpu.VMEM((nrow, 1), F32),
                pltpu.VMEM((nrow, 1), F32),
                pltpu.VMEM((nrow, hd2), F32),
            ],
        ),
        out_shape=jax.ShapeDtypeStruct((nb, N_HEADS, hd2), BF16),
        compiler_params=_cparams("parallel", "arbitrary"),
        name="attn_paged",
    )(page_table, slopes, qt, k_new, v_new, lambda_qk, subln_g,
      *([cache_kt] * pp), *([cache_v2] * pp))


def _lru_gates(xc, gw_ref, gb_ref, lam_ref):
    w = xc.shape[1]
    bw = w // LRU_BLOCKS
    g0, g1 = [], []
    for n in range(LRU_BLOCKS):
        xb = xc[:, n * bw:(n + 1) * bw].astype(BF16)
        g0.append(_dot(xb, gw_ref[0, n]))
        g1.append(_dot(xb, gw_ref[1, n]))
    r = _sigmoid(jnp.concatenate(g0, axis=-1) + gb_ref[0:1, :])
    i = _sigmoid(jnp.concatenate(g1, axis=-1) + gb_ref[1:2, :])
    neg_lam = -lam_ref[...]
    softplus = jnp.maximum(neg_lam, 0.0) + jnp.log1p(jnp.exp(-jnp.abs(neg_lam)))
    a = jnp.exp2(r * ((-LRU_C * LOG2_E) * softplus))
    d = 1.0 - a * a
    root = jnp.where(d > 0.0, d * lax.rsqrt(d), 0.0)
    u = root * (i * xc)
    return a, u


def _grouped(x):
    return x.reshape(x.shape[0] // 8, 8, x.shape[1])


def _shift_rows(x3, prev8, s, sub):
    rolled = pltpu.roll(jnp.concatenate([prev8[None], x3], axis=0), s, 1)
    return jnp.where(sub >= s, rolled[1:], rolled[:-1])


def _scan_groups(a3, u3, h_in):
    sub = lax.broadcasted_iota(jnp.int32, (1, 8, 1), 1)
    for s in (1, 2, 4):
        keep = sub >= s
        a_sh = jnp.where(keep, pltpu.roll(a3, s, 1), 1.0)
        u_sh = jnp.where(keep, pltpu.roll(u3, s, 1), 0.0)
        u3 = a3 * u_sh + u3
        a3 = a3 * a_sh
    out = []
    h_prev = h_in
    for g in range(a3.shape[0]):
        hg = u3[g] + a3[g] * h_prev
        out.append(hg)
        h_prev = hg[7:8, :]
    return jnp.concatenate(out, axis=0)


def _mixer_kernel(rgx_ref, rgg_ref, scb_ref, scc_ref, scx_ref, h0_ref, rgbuf_ref, scbuf_ref,
                  cw_ref, cb_ref, gw_ref, gb_ref, lam_ref, scw_ref,
                  yrg_ref, ysc_ref, hout_ref, rgbuf_out_ref, scbuf_out_ref,
                  prev_rg, prev_sc, hcarry, *, tt):
    t = pl.program_id(1)
    nrg = rgbuf_ref.shape[0]
    nsc = scbuf_ref.shape[0]
    w = rgx_ref.shape[1]
    sub = lax.broadcasted_iota(jnp.int32, (1, 8, 1), 1)

    @pl.when(t == 0)
    def _():
        prev_rg[...] = jnp.zeros(prev_rg.shape, F32)
        prev_sc[...] = jnp.zeros(prev_sc.shape, F32)
        prev_rg[8 - nrg:8, :] = rgbuf_ref[...]
        prev_sc[8 - nsc:8, :] = scbuf_ref[...]
        hcarry[...] = h0_ref[...]

    x = rgx_ref[...]
    x3 = _grouped(x)
    prev = prev_rg[...]
    xc3 = cb_ref[...] + cw_ref[nrg:nrg + 1, :] * x3
    for s in range(1, nrg + 1):
        xc3 = xc3 + cw_ref[nrg - s:nrg - s + 1, :] * _shift_rows(x3, prev, s, sub)
    prev_rg[...] = x[tt - 8:tt, :]
    a, u = _lru_gates(xc3.reshape(tt, w), gw_ref, gb_ref, lam_ref)
    h = _scan_groups(_grouped(a), _grouped(u), hcarry[...])
    hcarry[...] = h[tt - 1:tt, :]
    yrg_ref[...] = (h * _gelu_tanh(rgg_ref[...])).astype(yrg_ref.dtype)

    cx = scc_ref[...] * scx_ref[...]
    cx3 = _grouped(cx)
    prev = prev_sc[...]
    y3 = scw_ref[nsc:nsc + 1, :] * cx3
    for s in range(1, nsc + 1):
        y3 = y3 + scw_ref[nsc - s:nsc - s + 1, :] * _shift_rows(cx3, prev, s, sub)
    prev_sc[...] = cx[tt - 8:tt, :]
    ysc_ref[...] = (scb_ref[...] * y3.reshape(tt, w)).astype(ysc_ref.dtype)

    @pl.when(t == pl.num_programs(1) - 1)
    def _():
        hout_ref[...] = hcarry[...]
        rgbuf_out_ref[...] = prev_rg[8 - nrg:8, :]
        scbuf_out_ref[...] = prev_sc[8 - nsc:8, :]


def _mixer_prompt(zr, h0, rg_buf, sc_buf, rg_conv_w, rg_conv_b, gate_w, rg_gate_b, rg_lambda,
                  sc_conv_w, layer, batch, seq, tt):
    rows = zr.shape[0]
    w = h0.shape[-1]
    nt = seq // tt
    nrg, nsc = rg_buf.shape[1], sc_buf.shape[1]

    def zcol(cidx):
        return pl.BlockSpec((tt, w), lambda b, t: (b * nt + t, cidx))

    def per_batch(n):
        return pl.BlockSpec((None, n, w), lambda b, t: (b, 0, 0))

    def per_layer(n):
        return pl.BlockSpec((None, n, w), lambda b, t: (layer, 0, 0))

    bw = w // LRU_BLOCKS
    return pl.pallas_call(
        functools.partial(_mixer_kernel, tt=tt),
        grid=(batch, nt),
        in_specs=[zcol(0), zcol(1), zcol(2), zcol(3), zcol(4),
                  per_batch(1), per_batch(nrg), per_batch(nsc),
                  per_layer(nrg + 1), per_layer(1),
                  pl.BlockSpec((None, 2, LRU_BLOCKS, bw, bw), lambda b, t: (layer, 0, 0, 0, 0)),
                  per_layer(2), per_layer(1), per_layer(nsc + 1)],
        out_specs=[pl.BlockSpec((tt, w), lambda b, t: (b * nt + t, 0)),
                   pl.BlockSpec((tt, w), lambda b, t: (b * nt + t, 0)),
                   per_batch(1), per_batch(nrg), per_batch(nsc)],
        out_shape=[jax.ShapeDtypeStruct((rows, w), BF16),
                   jax.ShapeDtypeStruct((rows, w), BF16),
                   jax.ShapeDtypeStruct((batch, 1, w), F32),
                   jax.ShapeDtypeStruct((batch, nrg, w), F32),
                   jax.ShapeDtypeStruct((batch, nsc, w), F32)],
        scratch_shapes=[pltpu.VMEM((8, w), F32), pltpu.VMEM((8, w), F32),
                        pltpu.VMEM((1, w), F32)],
        compiler_params=_cparams("parallel", "arbitrary"),
        name="mixer_prompt",
    )(zr, zr, zr, zr, zr, h0, rg_buf, sc_buf, rg_conv_w, rg_conv_b, gate_w, rg_gate_b,
      rg_lambda, sc_conv_w)


def _mixer_step_kernel(rgx_ref, rgg_ref, scb_ref, scc_ref, scx_ref, h0_ref, rgbuf_ref, scbuf_ref,
                       cw_ref, cb_ref, gw_ref, gb_ref, lam_ref, scw_ref,
                       yrg_ref, ysc_ref, hout_ref, rgbuf_out_ref, scbuf_out_ref, *, nrg, nsc):
    w = h0_ref.shape[1]
    x = rgx_ref[...]
    xc = cb_ref[...] + cw_ref[nrg:nrg + 1, :] * x
    for j in range(nrg):
        xc = xc + cw_ref[j:j + 1, :] * rgbuf_ref[:, j * w:(j + 1) * w]
    a, u = _lru_gates(xc, gw_ref, gb_ref, lam_ref)
    h = a * h0_ref[...] + u
    hout_ref[...] = h
    yrg_ref[...] = (h * _gelu_tanh(rgg_ref[...])).astype(yrg_ref.dtype)
    for j in range(nrg - 1):
        rgbuf_out_ref[:, j * w:(j + 1) * w] = rgbuf_ref[:, (j + 1) * w:(j + 2) * w]
    rgbuf_out_ref[:, (nrg - 1) * w:] = x

    cx = scc_ref[...] * scx_ref[...]
    y = scw_ref[nsc:nsc + 1, :] * cx
    for j in range(nsc):
        y = y + scw_ref[j:j + 1, :] * scbuf_ref[:, j * w:(j + 1) * w]
    ysc_ref[...] = (scb_ref[...] * y).astype(ysc_ref.dtype)
    for j in range(nsc - 1):
        scbuf_out_ref[:, j * w:(j + 1) * w] = scbuf_ref[:, (j + 1) * w:(j + 2) * w]
    scbuf_out_ref[:, (nsc - 1) * w:] = cx


def _mixer_sample(zr, h0, rg_buf, sc_buf, rg_conv_w, rg_conv_b, gate_w, rg_gate_b, rg_lambda,
                  sc_conv_w, layer):
    nb, w = h0.shape
    nrg, nsc = rg_buf.shape[1], sc_buf.shape[1]
    bw = w // LRU_BLOCKS

    def zcol(cidx):
        return pl.BlockSpec((nb, w), lambda i: (0, cidx))

    def full(n):
        return pl.BlockSpec((nb, n * w), lambda i: (0, 0))

    def per_layer(n):
        return pl.BlockSpec((None, n, w), lambda i: (layer, 0, 0))

    outs = pl.pallas_call(
        functools.partial(_mixer_step_kernel, nrg=nrg, nsc=nsc),
        grid=(1,),
        in_specs=[zcol(0), zcol(1), zcol(2), zcol(3), zcol(4),
                  full(1), full(nrg), full(nsc),
                  per_layer(nrg + 1), per_layer(1),
                  pl.BlockSpec((None, 2, LRU_BLOCKS, bw, bw), lambda i: (layer, 0, 0, 0, 0)),
                  per_layer(2), per_layer(1), per_layer(nsc + 1)],
        out_specs=[full(1), full(1), full(1), full(nrg), full(nsc)],
        out_shape=[jax.ShapeDtypeStruct((nb, w), BF16),
                   jax.ShapeDtypeStruct((nb, w), BF16),
                   jax.ShapeDtypeStruct((nb, w), F32),
                   jax.ShapeDtypeStruct((nb, nrg * w), F32),
                   jax.ShapeDtypeStruct((nb, nsc * w), F32)],
        compiler_params=_cparams("arbitrary"),
        name="mixer_sample",
    )(zr, zr, zr, zr, zr, h0, rg_buf.reshape(nb, nrg * w), sc_buf.reshape(nb, nsc * w),
      rg_conv_w, rg_conv_b, gate_w, rg_gate_b, rg_lambda, sc_conv_w)
    y_rg, y_sc, h, rgb, scb = outs
    return y_rg, y_sc, h, rgb.reshape(nb, nrg, w), scb.reshape(nb, nsc, w)


def _merge_kernel(x_ref, o_ref, yrg_ref, ysc_ref, gz0_ref, gz1_ref, gz2_ref, g_ref,
                  wa_ref, wr_ref, ws_ref, wo_ref, out_ref):
    m = _sigmoid(gz0_ref[...]) * _dot(o_ref[...], wa_ref[...])
    m = m + _sigmoid(gz1_ref[...]) * _dot(yrg_ref[...], wr_ref[...])
    m = m + _sigmoid(gz2_ref[...]) * _dot(ysc_ref[...], ws_ref[...])
    y = _dot(m.astype(BF16), wo_ref[...])
    out_ref[...] = x_ref[...] + _rms(y, g_ref[3:4, :])


def _merge(x, o, y_rg, y_sc, zr, norm_g, w_a, w_r, w_s, w_o, layer, tm, gz_col):
    rows, d = x.shape

    def rowblk(cidx=0):
        return pl.BlockSpec((tm, d), lambda i: (i, cidx))

    def wspec(arr):
        return _resident((None,) + arr.shape[1:], lambda i: (layer, 0, 0))

    return pl.pallas_call(
        _merge_kernel,
        grid=(rows // tm,),
        in_specs=[rowblk(), rowblk(), rowblk(), rowblk(),
                  rowblk(gz_col), rowblk(gz_col + 1), rowblk(gz_col + 2),
                  _resident((None, 6, d), lambda i: (layer, 0, 0)),
                  wspec(w_a), wspec(w_r), wspec(w_s), wspec(w_o)],
        out_specs=rowblk(),
        out_shape=jax.ShapeDtypeStruct((rows, d), F32),
        compiler_params=_cparams("parallel"),
        name="merge",
    )(x, o, y_rg, y_sc, zr, zr, zr, norm_g, w_a, w_r, w_s, w_o)


def _pick_tile(n, pref):
    t = min(n, pref)
    while n % t:
        t //= 2
    return t


def kernel(x_prompt, x_sample, cache_k, cache_v, page_table, state_rglru_h, state_rglru_conv, state_sconv, norm_g, w_ffn_up, w_ffn_down, w_in, lambda_qk, subln_g, rg_conv_w, rg_conv_b, rg_gate_w, rg_gate_b, rg_lambda, sc_conv_w, w_branch_attn, w_branch_rg, w_branch_sc, w_out):
    batch, seq, d = x_prompt.shape
    nb, dec_seq, _ = x_sample.shape
    assert dec_seq == 1, "the sample group carries one new token per sequence"
    depth = w_in.shape[0]
    n_pool, page = cache_k.shape[1], cache_k.shape[2]
    head_dim = cache_k.shape[-1]
    hd2 = 2 * head_dim
    qw = N_HEADS * hd2
    kw = N_KV_HEADS * hd2
    w = state_rglru_h.shape[-1]
    assert qw == d and w == d and qw + 2 * kw == 2 * d
    scale = head_dim ** -0.5
    assert math.log2(scale) == round(math.log2(scale)), "q pre-scaling must be exact in bf16"
    n_pages = page_table.shape[1]
    nrg, nsc = state_rglru_conv.shape[2], state_sconv.shape[2]

    w_up_b = w_ffn_up.astype(BF16)
    w_dn_b = w_ffn_down.astype(BF16)
    w_in_b = w_in.astype(BF16)
    w_a_b = w_branch_attn.astype(BF16)
    w_r_b = w_branch_rg.astype(BF16)
    w_s_b = w_branch_sc.astype(BF16)
    w_o_b = w_out.astype(BF16)
    gate_w_b = rg_gate_w.astype(BF16)

    slopes_np = np.float32(2.0) ** (-8.0 * np.arange(1, N_HEADS + 1, dtype=np.float32) / N_HEADS)
    assert all(math.frexp(float(s))[0] == 0.5 for s in slopes_np), "slopes must be bf16-exact"
    slopes = jnp.asarray(slopes_np, F32)
    cache_kt = jnp.transpose(cache_k, (0, 1, 3, 4, 5, 2)).reshape(depth, n_pool, kw, page)
    cache_v2 = cache_v.reshape(depth, n_pool, page * N_KV_HEADS, hd2)
    rg_conv_b3 = rg_conv_b.reshape(depth, 1, w)
    rg_lambda3 = rg_lambda.reshape(depth, 1, w)
    subln_g3 = subln_g.reshape(depth, 1, hd2)
    eye_kv = jnp.eye(N_KV_HEADS, dtype=BF16)
    eye_m = jnp.eye(2, dtype=BF16)

    tm = _pick_tile(batch * seq, 512)
    tq = _pick_tile(seq, 256)
    tt = _pick_tile(seq, 256)
    pps = _pick_tile(n_pages, 16)
    gz_col = 5

    xp = x_prompt.reshape(batch * seq, d)
    xs = x_sample.reshape(nb, d)
    zeros_h = jnp.zeros((batch, 1, w), F32)
    zeros_rg = jnp.zeros((batch, nrg, w), F32)
    zeros_sc = jnp.zeros((batch, nsc, w), F32)

    p_states, s_states = [], []
    for l in range(depth):
        lam_init = 0.8 - 0.6 * math.exp(-0.3 * l)
        dense = dict(norm_g=norm_g, layer=l)

        xp = _ffn(xp, norm_g, w_up_b, w_dn_b, l, 0, tm)
        q, k, v, zr = _inproj(xp, norm_g, w_in_b, l, tm, qw, kw, scale)
        o = _attn_prompt(q, k, v, slopes, lambda_qk, subln_g3, l, batch, seq, tq, lam_init)
        y_rg, y_sc, h_p, rgb_p, scb_p = _mixer_prompt(
            zr, zeros_h, zeros_rg, zeros_sc, rg_conv_w, rg_conv_b3, gate_w_b, rg_gate_b,
            rg_lambda3, sc_conv_w, l, batch, seq, tt)
        xp = _merge(xp, o, y_rg, y_sc, zr, norm_g, w_a_b, w_r_b, w_s_b, w_o_b, l, tm, gz_col)
        xp = _ffn(xp, norm_g, w_up_b, w_dn_b, l, 1, tm)
        p_states.append((k.reshape(batch, seq, N_KV_HEADS, 2, head_dim),
                         v.reshape(batch, seq, N_KV_HEADS, hd2),
                         h_p.reshape(batch, w), rgb_p, scb_p))

        xs = _ffn(xs, norm_g, w_up_b, w_dn_b, l, 0, nb)
        q, k, v, zr = _inproj(xs, norm_g, w_in_b, l, nb, qw, kw, scale)
        q5 = q.reshape(nb, N_KV_HEADS, HEAD_GROUP, 2, head_dim)
        qt = jnp.einsum('bkgmd,kK,mM->bkgmKMd', q5, eye_kv, eye_m).reshape(nb, 2 * N_HEADS, kw)
        o = _attn_paged(qt, k.reshape(nb, 1, kw), v.reshape(nb, 1, kw), cache_kt, cache_v2,
                        page_table, slopes, lambda_qk, subln_g3, l, pps, lam_init)
        o = o.reshape(nb, qw)
        y_rg, y_sc, h_s, rgb_s, scb_s = _mixer_sample(
            zr, state_rglru_h[l], state_rglru_conv[l], state_sconv[l], rg_conv_w, rg_conv_b3,
            gate_w_b, rg_gate_b, rg_lambda3, sc_conv_w, l)
        xs = _merge(xs, o, y_rg, y_sc, zr, norm_g, w_a_b, w_r_b, w_s_b, w_o_b, l, nb, gz_col)
        xs = _ffn(xs, norm_g, w_up_b, w_dn_b, l, 1, nb)
        s_states.append((k.reshape(nb, 1, N_KV_HEADS, 2, head_dim),
                         v.reshape(nb, 1, N_KV_HEADS, hd2), h_s, rgb_s, scb_s))

    k_p, v_p, h_p, rgc_p, sc_p = [jnp.stack(s, axis=0) for s in zip(*p_states)]
    k_s, v_s, h_s, rgc_s, sc_s = [jnp.stack(s, axis=0) for s in zip(*s_states)]
    return (xp.reshape(batch, seq, d), xs.reshape(nb, 1, d), k_p, v_p, h_p, rgc_p, sc_p,
            k_s, v_s, h_s, rgc_s, sc_s)
```

```python
import functools
import math

import numpy as np
import jax
import jax.numpy as jnp
from jax import lax
from jax.experimental import pallas as pl
from jax.experimental.pallas import tpu as pltpu

F32 = jnp.float32
BF16 = jnp.bfloat16

NORM_EPS = 1e-6
LRU_C = 8.0
N_HEADS = 8
N_KV_HEADS = 4
HEAD_GROUP = N_HEADS // N_KV_HEADS
LRU_BLOCKS = 8
N_BRANCHES = 3
MASK_VALUE = -1e30
LOG2_E = math.log2(math.e)
VMEM_LIMIT_BYTES = 56 * 1024 * 1024


def _cparams(*sem):
    return pltpu.CompilerParams(dimension_semantics=sem, vmem_limit_bytes=VMEM_LIMIT_BYTES)


def _rms(x, g):
    return x * lax.rsqrt(jnp.mean(x * x, axis=-1, keepdims=True) + NORM_EPS) * g


def _dot(a, b):
    return jnp.dot(a, b, preferred_element_type=F32)


def _dot_nt(a, b):
    return lax.dot_general(a, b, (((1,), (1,)), ((), ())), preferred_element_type=F32)


def _sigmoid(x):
    return 0.5 * jnp.tanh(0.5 * x) + 0.5


def _gelu_tanh(x):
    c = math.sqrt(2.0 / math.pi)
    return 0.5 * x * (1.0 + jnp.tanh(c * (x + 0.044715 * (x * x * x))))


def _resident(shape, index_map):
    return pl.BlockSpec(shape, index_map, pipeline_mode=pl.Buffered(1))


def _ffn_kernel(x_ref, g_ref, wup_ref, wdn_ref, o_ref, *, d_ff, chunks, g_pre, g_post):
    x = x_ref[...]
    h = _rms(x, g_ref[g_pre:g_pre + 1, :]).astype(BF16)
    acc = None
    for c0, cw in chunks:
        gate = _dot(h, wup_ref[:, c0:c0 + cw])
        up = _dot(h, wup_ref[:, d_ff + c0:d_ff + c0 + cw])
        act = (gate * _sigmoid(gate) * up).astype(BF16)
        part = _dot(act, wdn_ref[c0:c0 + cw, :])
        acc = part if acc is None else acc + part
    o_ref[...] = x + 0.5 * _rms(acc, g_ref[g_post:g_post + 1, :])


def _ffn(x, norm_g, w_up, w_dn, layer, which, tm):
    rows, d = x.shape
    d_ff = w_dn.shape[2]
    chunk = 1024
    chunks = tuple((c0, min(chunk, d_ff - c0)) for c0 in range(0, d_ff, chunk))
    kern = functools.partial(_ffn_kernel, d_ff=d_ff, chunks=chunks,
                             g_pre=0 if which == 0 else 4, g_post=1 if which == 0 else 5)
    return pl.pallas_call(
        kern,
        grid=(rows // tm,),
        in_specs=[
            pl.BlockSpec((tm, d), lambda i: (i, 0)),
            _resident((None, 6, d), lambda i: (layer, 0, 0)),
            _resident((None, None, d, 2 * d_ff), lambda i: (layer, which, 0, 0)),
            _resident((None, None, d_ff, d), lambda i: (layer, which, 0, 0)),
        ],
        out_specs=pl.BlockSpec((tm, d), lambda i: (i, 0)),
        out_shape=jax.ShapeDtypeStruct((rows, d), F32),
        compiler_params=_cparams("parallel"),
        name=f"ffn{which}",
    )(x, norm_g, w_up, w_dn)


def _qkv_kernel(x_ref, g_ref, w_ref, q_ref, k_ref, v_ref, *, qw, kw, scale):
    h = _rms(x_ref[...], g_ref[2:3, :]).astype(BF16)
    z = _dot(h, w_ref[...])
    q_ref[...] = (z[:, :qw] * scale).astype(BF16)
    k_ref[...] = z[:, qw:qw + kw]
    v_ref[...] = z[:, qw + kw:]


def _zrest_kernel(x_ref, g_ref, w_ref, z_ref):
    h = _rms(x_ref[...], g_ref[2:3, :]).astype(BF16)
    z_ref[...] = _dot(h, w_ref[...])


def _inproj(x, norm_g, w_in, layer, tm, qw, kw, scale):
    rows, d = x.shape
    in_w = w_in.shape[2]
    cw = qw + 2 * kw
    n_rest = (in_w - cw) // cw
    assert cw * (n_rest + 1) == in_w
    q, k, v = pl.pallas_call(
        functools.partial(_qkv_kernel, qw=qw, kw=kw, scale=scale),
        grid=(rows // tm,),
        in_specs=[
            pl.BlockSpec((tm, d), lambda i: (i, 0)),
            _resident((None, 6, d), lambda i: (layer, 0, 0)),
            _resident((None, d, cw), lambda i: (layer, 0, 0)),
        ],
        out_specs=[
            pl.BlockSpec((tm, qw), lambda i: (i, 0)),
            pl.BlockSpec((tm, kw), lambda i: (i, 0)),
            pl.BlockSpec((tm, kw), lambda i: (i, 0)),
        ],
        out_shape=[
            jax.ShapeDtypeStruct((rows, qw), BF16),
            jax.ShapeDtypeStruct((rows, kw), F32),
            jax.ShapeDtypeStruct((rows, kw), F32),
        ],
        compiler_params=_cparams("parallel"),
        name="inproj_qkv",
    )(x, norm_g, w_in)
    tr = _pick_tile(rows, 2 * tm)
    zr = pl.pallas_call(
        _zrest_kernel,
        grid=(n_rest, rows // tr),
        in_specs=[
            pl.BlockSpec((tr, d), lambda j, i: (i, 0)),
            _resident((None, 6, d), lambda j, i: (layer, 0, 0)),
            pl.BlockSpec((None, d, cw), lambda j, i: (layer, 0, j + 1)),
        ],
        out_specs=pl.BlockSpec((tr, cw), lambda j, i: (i, j)),
        out_shape=jax.ShapeDtypeStruct((rows, in_w - cw), F32),
        compiler_params=_cparams("parallel", "parallel"),
        name="inproj_rest",
    )(x, norm_g, w_in)
    return q, k, v, zr


def _lambda_full(lqk_ref, lam_init):
    s01 = jnp.sum(lqk_ref[0:1, :] * lqk_ref[1:2, :], axis=-1, keepdims=True)
    s23 = jnp.sum(lqk_ref[2:3, :] * lqk_ref[3:4, :], axis=-1, keepdims=True)
    return jnp.exp(s01) - jnp.exp(s23) + lam_init


def _subln(o, sg, lam_init):
    return _rms(o, sg) * (1.0 - lam_init)


def _loop_unrolled(n, fn, unroll):
    def body(jj, carry):
        for u in range(unroll):
            fn(unroll * jj + u)
        return carry

    lax.fori_loop(0, n // unroll, body, 0)
    if unroll == 4:
        base = (n // 4) * 4

        @pl.when(n % 4 >= 2)
        def _():
            fn(base)
            fn(base + 1)

    @pl.when(n % 2 == 1)
    def _():
        fn(n - 1)


def _attn_kernel(slopes_ref, q_ref, k_ref, v_ref, lqk_ref, sg_ref, o_ref,
                 kaug, vbf, s_scr, mx_scr, ls_scr, acc_scr, *, tq, lam_init):
    kvh = pl.program_id(1)
    seq, hd2 = k_ref.shape
    hd = hd2 // 2
    tk = tq
    nc = tk // 128
    nq = seq // tq
    lane = lax.broadcasted_iota(jnp.int32, (1, hd2), 1)

    pos = lax.broadcasted_iota(jnp.int32, (seq, 1), 0)
    lo = pos & 7
    hi = (pos - lo).astype(F32)
    lo = lo.astype(F32)
    k = k_ref[...]
    kaug[0] = jnp.where(lane < hd, k, jnp.where(lane == hd, hi,
                        jnp.where(lane == hd + 1, lo, 0.0))).astype(BF16)
    kaug[1] = jnp.where(lane >= hd, k, jnp.where(lane == 0, hi,
                        jnp.where(lane == 1, lo, 0.0))).astype(BF16)
    vbf[...] = v_ref[...].astype(BF16)

    row = lax.broadcasted_iota(jnp.int32, (2 * tq, 1), 0)
    slope = jnp.where(row < tq, slopes_ref[2 * kvh], slopes_ref[2 * kvh + 1])
    diagonal_visible = (lax.broadcasted_iota(jnp.int32, (1, tk), 1)
                        <= jnp.where(row < tq, row, row - tq))
    lam = _lambda_full(lqk_ref, lam_init)

    def q_tile(i):
        q = q_ref[pl.ds(pl.multiple_of(i * tq, tq), tq), :].astype(F32)
        qs = jnp.concatenate([q[:, :hd2], q[:, hd2:]], axis=0)
        return (jnp.where(lane < hd, qs, jnp.where(lane < hd + 2, slope, 0.0)).astype(BF16),
                jnp.where(lane >= hd, qs, jnp.where(lane < 2, slope, 0.0)).astype(BF16))

    def scores(qm, buf, j, on_diagonal):
        k0 = pl.multiple_of(j * tk, tk)
        for mm in range(2):
            s = _dot_nt(qm[mm], kaug[mm, pl.ds(k0, tk), :]) * LOG2_E
            if on_diagonal:
                s = jnp.where(diagonal_visible, s, MASK_VALUE)
            s_scr[buf, mm, j] = s
            mx = mx_scr[buf, mm]
            for c in range(nc):
                mx = jnp.maximum(mx, s[:, c * 128:(c + 1) * 128])
            mx_scr[buf, mm] = mx

    def weigh(buf, j):
        k0 = pl.multiple_of(j * tk, tk)
        vb = vbf[pl.ds(k0, tk), :]
        for mm in range(2):
            s = s_scr[buf, mm, j]
            mb = mx_scr[buf, mm]
            ps = [jnp.exp2(s[:, c * 128:(c + 1) * 128] - mb) for c in range(nc)]
            ls = ls_scr[mm]
            for c in range(nc):
                ls = ls + ps[c]
            ls_scr[mm] = ls
            p = jnp.concatenate(ps, axis=-1).astype(BF16)
            acc_scr[mm] = acc_scr[mm] + _dot(p, vb)

    def open_scores(buf):
        mx_scr[buf] = jnp.full(mx_scr.shape[1:], MASK_VALUE, F32)

    def close_scores(buf):
        for mm in range(2):
            mx_scr[buf, mm] = jnp.broadcast_to(
                jnp.max(mx_scr[buf, mm], axis=-1, keepdims=True), mx_scr.shape[2:])
        ls_scr[...] = jnp.zeros(ls_scr.shape, F32)
        acc_scr[...] = jnp.zeros(acc_scr.shape, F32)

    def emit(i):
        l0 = jnp.sum(ls_scr[0], axis=-1, keepdims=True)
        l1 = jnp.sum(ls_scr[1], axis=-1, keepdims=True)
        o = acc_scr[0] / l0 - lam * (acc_scr[1] / l1)
        o = _subln(o, sg_ref[...], lam_init).astype(o_ref.dtype)
        r0 = pl.multiple_of(i * tq, tq)
        o_ref[pl.ds(r0, tq), :hd2] = o[:tq]
        o_ref[pl.ds(r0, tq), hd2:] = o[tq:]

    open_scores(0)
    scores(q_tile(0), 0, 0, True)

    def q_step(i, carry):
        cur = i % 2
        nxt = 1 - cur
        close_scores(cur)
        q_next = q_tile(i + 1)
        open_scores(nxt)

        def both(j):
            weigh(cur, j)
            scores(q_next, nxt, j, False)

        _loop_unrolled(i + 1, both, 2)
        scores(q_next, nxt, i + 1, True)
        emit(i)
        return carry

    lax.fori_loop(0, nq - 1, q_step, 0)
    last = (nq - 1) % 2
    close_scores(last)
    _loop_unrolled(nq, lambda j: weigh(last, j), 4)
    emit(nq - 1)


def _attn_prompt(q, k, v, slopes, lambda_qk, subln_g, layer, batch, seq, tq, lam_init):
    rows, qw = q.shape
    hd2 = k.shape[1] // N_KV_HEADS
    nq = seq // tq
    assert tq % 128 == 0 and seq % tq == 0
    kern = functools.partial(_attn_kernel, tq=tq, lam_init=lam_init)
    return pl.pallas_call(
        kern,
        grid_spec=pltpu.PrefetchScalarGridSpec(
            num_scalar_prefetch=1,
            grid=(batch, N_KV_HEADS),
            in_specs=[
                pl.BlockSpec((seq, HEAD_GROUP * hd2), lambda b, h, s: (b, h)),
                pl.BlockSpec((seq, hd2), lambda b, h, s: (b, h)),
                pl.BlockSpec((seq, hd2), lambda b, h, s: (b, h)),
                pl.BlockSpec((None, 4, hd2 // 2), lambda b, h, s: (layer, 0, 0)),
                pl.BlockSpec((None, 1, hd2), lambda b, h, s: (layer, 0, 0)),
            ],
            out_specs=pl.BlockSpec((seq, HEAD_GROUP * hd2), lambda b, h, s: (b, h)),
            scratch_shapes=[
                pltpu.VMEM((2, seq, hd2), BF16),
                pltpu.VMEM((seq, hd2), BF16),
                pltpu.VMEM((2, 2, nq, 2 * tq, tq), F32),
                pltpu.VMEM((2, 2, 2 * tq, 128), F32),
                pltpu.VMEM((2, 2 * tq, 128), F32),
                pltpu.VMEM((2, 2 * tq, hd2), F32),
            ],
        ),
        out_shape=jax.ShapeDtypeStruct((rows, qw), BF16),
        compiler_params=_cparams("parallel", "parallel"),
        name="attn_prompt",
    )(slopes, q, k, v, lambda_qk, subln_g)


def _paged_kernel(pt_ref, slopes_ref, qt_ref, ks_ref, vs_ref, lqk_ref, sg_ref, *refs,
                  pages_per_step, page, past, lam_init):
    del pt_ref
    pp = pages_per_step
    kt_refs = refs[:pp]
    v_refs = refs[pp:2 * pp]
    o_ref = refs[2 * pp]
    m_ref, l_ref, acc_ref = refs[2 * pp + 1:]
    c = pl.program_id(1)
    nrow = qt_ref.shape[0]
    hd2 = sg_ref.shape[1]
    qt = qt_ref[...]
    row = lax.broadcasted_iota(jnp.int32, (nrow, 1), 0)
    row_kv = row // (2 * HEAD_GROUP)
    head = row // 2
    slope = jnp.zeros((nrow, 1), F32)
    for h in range(N_HEADS):
        slope = jnp.where(head == h, slopes_ref[h], slope)

    @pl.when(c == 0)
    def _():
        ks = ks_ref[...].astype(BF16).astype(F32)
        m_ref[...] = jnp.sum(qt.astype(F32) * ks, axis=-1, keepdims=True)
        l_ref[...] = jnp.ones(l_ref.shape, F32)
        vs = vs_ref[...].astype(BF16).astype(F32)
        a = jnp.zeros(acc_ref.shape, F32)
        for kv in range(N_KV_HEADS):
            a = jnp.where(row_kv == kv, vs[:, kv * hd2:(kv + 1) * hd2], a)
        acc_ref[...] = a

    s_parts = []
    for r in range(pp):
        kpos = (c * pp + r) * page + lax.broadcasted_iota(jnp.int32, (1, page), 1)
        dist = (past - kpos).astype(F32)
        s_parts.append(_dot(qt, kt_refs[r][...].astype(BF16)) - slope * dist)
    s = jnp.concatenate(s_parts, axis=-1)
    m_old = m_ref[...]
    m_new = jnp.maximum(m_old, jnp.max(s, axis=-1, keepdims=True))
    p = jnp.exp(s - m_new)
    corr = jnp.exp(m_old - m_new)
    l_ref[...] = corr * l_ref[...] + jnp.sum(p, axis=-1, keepdims=True)
    pv = None
    for kv in range(N_KV_HEADS):
        pk = jnp.where(row_kv == kv, p, 0.0).astype(BF16)
        for r in range(pp):
            v_kv = v_refs[r][pl.ds(kv, page, stride=N_KV_HEADS), :].astype(BF16)
            part = _dot(pk[:, r * page:(r + 1) * page], v_kv)
            pv = part if pv is None else pv + part
    acc_ref[...] = corr * acc_ref[...] + pv
    m_ref[...] = m_new

    @pl.when(c == pl.num_programs(1) - 1)
    def _():
        lam = _lambda_full(lqk_ref, lam_init)
        acc_ref[...] = acc_ref[...] / l_ref[...]
        a0 = acc_ref[pl.ds(0, N_HEADS, stride=2), :]
        a1 = acc_ref[pl.ds(1, N_HEADS, stride=2), :]
        o_ref[...] = _subln(a0 - lam * a1, sg_ref[...], lam_init).astype(o_ref.dtype)


def _attn_paged(qt, k_new, v_new, cache_kt, cache_v2, page_table, slopes, lambda_qk, subln_g,
                layer, pages_per_step, lam_init):
    nb, nrow, kvw = qt.shape
    page = cache_kt.shape[3]
    n_pages = page_table.shape[1]
    pp = pages_per_step
    hd2 = kvw // N_KV_HEADS
    assert cache_kt.shape[2] == kvw and cache_v2.shape[2:] == (page * N_KV_HEADS, hd2)
    kern = functools.partial(_paged_kernel, pages_per_step=pp, page=page,
                             past=n_pages * page, lam_init=lam_init)

    def page_spec(arr, r):
        return pl.BlockSpec((None, None) + arr.shape[2:],
                            lambda b, c, pt, s: (layer, pt[b, c * pp + r], 0, 0))

    return pl.pallas_call(
        kern,
        grid_spec=pltpu.PrefetchScalarGridSpec(
            num_scalar_prefetch=2,
            grid=(nb, n_pages // pp),
            in_specs=[
                pl.BlockSpec((None, nrow, kvw), lambda b, c, pt, s: (b, 0, 0)),
                pl.BlockSpec((None, 1, kvw), lambda b, c, pt, s: (b, 0, 0)),
                pl.BlockSpec((None, 1, kvw), lambda b, c, pt, s: (b, 0, 0)),
                pl.BlockSpec((None, 4, hd2 // 2), lambda b, c, pt, s: (layer, 0, 0)),
                pl.BlockSpec((None, 1, hd2), lambda b, c, pt, s: (layer, 0, 0)),
            ] + [page_spec(cache_kt, r) for r in range(pp)]
              + [page_spec(cache_v2, r) for r in range(pp)],
            out_specs=pl.BlockSpec((None, N_HEADS, hd2), lambda b, c, pt, s: (b, 0, 0)),
            scratch_shapes=[
                pltpu.VMEM((nrow, 1), F32),
                pltpu.VMEM((nrow, 1), F32),
                pltpu.VMEM((nrow, hd2), F32),
            ],
        ),
        out_shape=jax.ShapeDtypeStruct((nb, N_HEADS, hd2), BF16),
        compiler_params=_cparams("parallel", "arbitrary"),
        name="attn_paged",
    )(page_table, slopes, qt, k_new, v_new, lambda_qk, subln_g,
      *([cache_kt] * pp), *([cache_v2] * pp))


def _lru_gates(xc, gw_ref, gb_ref, lam_ref):
    w = xc.shape[1]
    bw = w // LRU_BLOCKS
    g0, g1 = [], []
    for n in range(LRU_BLOCKS):
        xb = xc[:, n * bw:(n + 1) * bw].astype(BF16)
        g0.append(_dot(xb, gw_ref[0, n]))
        g1.append(_dot(xb, gw_ref[1, n]))
    r = _sigmoid(jnp.concatenate(g0, axis=-1) + gb_ref[0:1, :])
    i = _sigmoid(jnp.concatenate(g1, axis=-1) + gb_ref[1:2, :])
    neg_lam = -lam_ref[...]
    softplus = jnp.maximum(neg_lam, 0.0) + jnp.log1p(jnp.exp(-jnp.abs(neg_lam)))
    a = jnp.exp2(r * ((-LRU_C * LOG2_E) * softplus))
    d = 1.0 - a * a
    root = jnp.where(d > 0.0, d * lax.rsqrt(d), 0.0)
    u = root * (i * xc)
    return a, u


def _grouped(x):
    return x.reshape(x.shape[0] // 8, 8, x.shape[1])


def _shift_rows(x3, prev8, s, sub):
    rolled = pltpu.roll(jnp.concatenate([prev8[None], x3], axis=0), s, 1)
    return jnp.where(sub >= s, rolled[1:], rolled[:-1])


def _scan_groups(a3, u3, h_in):
    sub = lax.broadcasted_iota(jnp.int32, (1, 8, 1), 1)
    for s in (1, 2, 4):
        keep = sub >= s
        a_sh = jnp.where(keep, pltpu.roll(a3, s, 1), 1.0)
        u_sh = jnp.where(keep, pltpu.roll(u3, s, 1), 0.0)
        u3 = a3 * u_sh + u3
        a3 = a3 * a_sh
    out = []
    h_prev = h_in
    for g in range(a3.shape[0]):
        hg = u3[g] + a3[g] * h_prev
        out.append(hg)
        h_prev = hg[7:8, :]
    return jnp.concatenate(out, axis=0)


def _mixer_kernel(rgx_ref, rgg_ref, scb_ref, scc_ref, scx_ref, h0_ref, rgbuf_ref, scbuf_ref,
                  cw_ref, cb_ref, gw_ref, gb_ref, lam_ref, scw_ref,
                  yrg_ref, ysc_ref, hout_ref, rgbuf_out_ref, scbuf_out_ref,
                  prev_rg, prev_sc, hcarry, *, tt):
    t = pl.program_id(1)
    nrg = rgbuf_ref.shape[0]
    nsc = scbuf_ref.shape[0]
    w = rgx_ref.shape[1]
    sub = lax.broadcasted_iota(jnp.int32, (1, 8, 1), 1)

    @pl.when(t == 0)
    def _():
        prev_rg[...] = jnp.zeros(prev_rg.shape, F32)
        prev_sc[...] = jnp.zeros(prev_sc.shape, F32)
        prev_rg[8 - nrg:8, :] = rgbuf_ref[...]
        prev_sc[8 - nsc:8, :] = scbuf_ref[...]
        hcarry[...] = h0_ref[...]

    x = rgx_ref[...]
    x3 = _grouped(x)
    prev = prev_rg[...]
    xc3 = cb_ref[...] + cw_ref[nrg:nrg + 1, :] * x3
    for s in range(1, nrg + 1):
        xc3 = xc3 + cw_ref[nrg - s:nrg - s + 1, :] * _shift_rows(x3, prev, s, sub)
    prev_rg[...] = x[tt - 8:tt, :]
    a, u = _lru_gates(xc3.reshape(tt, w), gw_ref, gb_ref, lam_ref)
    h = _scan_groups(_grouped(a), _grouped(u), hcarry[...])
    hcarry[...] = h[tt - 1:tt, :]
    yrg_ref[...] = (h * _gelu_tanh(rgg_ref[...])).astype(yrg_ref.dtype)

    cx = scc_ref[...] * scx_ref[...]
    cx3 = _grouped(cx)
    prev = prev_sc[...]
    y3 = scw_ref[nsc:nsc + 1, :] * cx3
    for s in range(1, nsc + 1):
        y3 = y3 + scw_ref[nsc - s:nsc - s + 1, :] * _shift_rows(cx3, prev, s, sub)
    prev_sc[...] = cx[tt - 8:tt, :]
    ysc_ref[...] = (scb_ref[...] * y3.reshape(tt, w)).astype(ysc_ref.dtype)

    @pl.when(t == pl.num_programs(1) - 1)
    def _():
        hout_ref[...] = hcarry[...]
        rgbuf_out_ref[...] = prev_rg[8 - nrg:8, :]
        scbuf_out_ref[...] = prev_sc[8 - nsc:8, :]


def _mixer_prompt(zr, h0, rg_buf, sc_buf, rg_conv_w, rg_conv_b, gate_w, rg_gate_b, rg_lambda,
                  sc_conv_w, layer, batch, seq, tt):
    rows = zr.shape[0]
    w = h0.shape[-1]
    nt = seq // tt
    nrg, nsc = rg_buf.shape[1], sc_buf.shape[1]

    def zcol(cidx):
        return pl.BlockSpec((tt, w), lambda b, t: (b * nt + t, cidx))

    def per_batch(n):
        return pl.BlockSpec((None, n, w), lambda b, t: (b, 0, 0))

    def per_layer(n):
        return pl.BlockSpec((None, n, w), lambda b, t: (layer, 0, 0))

    bw = w // LRU_BLOCKS
    return pl.pallas_call(
        functools.partial(_mixer_kernel, tt=tt),
        grid=(batch, nt),
        in_specs=[zcol(0), zcol(1), zcol(2), zcol(3), zcol(4),
                  per_batch(1), per_batch(nrg), per_batch(nsc),
                  per_layer(nrg + 1), per_layer(1),
                  pl.BlockSpec((None, 2, LRU_BLOCKS, bw, bw), lambda b, t: (layer, 0, 0, 0, 0)),
                  per_layer(2), per_layer(1), per_layer(nsc + 1)],
        out_specs=[pl.BlockSpec((tt, w), lambda b, t: (b * nt + t, 0)),
                   pl.BlockSpec((tt, w), lambda b, t: (b * nt + t, 0)),
                   per_batch(1), per_batch(nrg), per_batch(nsc)],
        out_shape=[jax.ShapeDtypeStruct((rows, w), BF16),
                   jax.ShapeDtypeStruct((rows, w), BF16),
                   jax.ShapeDtypeStruct((batch, 1, w), F32),
                   jax.ShapeDtypeStruct((batch, nrg, w), F32),
                   jax.ShapeDtypeStruct((batch, nsc, w), F32)],
        scratch_shapes=[pltpu.VMEM((8, w), F32), pltpu.VMEM((8, w), F32),
                        pltpu.VMEM((1, w), F32)],
        compiler_params=_cparams("parallel", "arbitrary"),
        name="mixer_prompt",
    )(zr, zr, zr, zr, zr, h0, rg_buf, sc_buf, rg_conv_w, rg_conv_b, gate_w, rg_gate_b,
      rg_lambda, sc_conv_w)


def _mixer_step_kernel(rgx_ref, rgg_ref, scb_ref, scc_ref, scx_ref, h0_ref, rgbuf_ref, scbuf_ref,
                       cw_ref, cb_ref, gw_ref, gb_ref, lam_ref, scw_ref,
                       yrg_ref, ysc_ref, hout_ref, rgbuf_out_ref, scbuf_out_ref, *, nrg, nsc):
    w = h0_ref.shape[1]
    x = rgx_ref[...]
    xc = cb_ref[...] + cw_ref[nrg:nrg + 1, :] * x
    for j in range(nrg):
        xc = xc + cw_ref[j:j + 1, :] * rgbuf_ref[:, j * w:(j + 1) * w]
    a, u = _lru_gates(xc, gw_ref, gb_ref, lam_ref)
    h = a * h0_ref[...] + u
    hout_ref[...] = h
    yrg_ref[...] = (h * _gelu_tanh(rgg_ref[...])).astype(yrg_ref.dtype)
    for j in range(nrg - 1):
        rgbuf_out_ref[:, j * w:(j + 1) * w] = rgbuf_ref[:, (j + 1) * w:(j + 2) * w]
    rgbuf_out_ref[:, (nrg - 1) * w:] = x

    cx = scc_ref[...] * scx_ref[...]
    y = scw_ref[nsc:nsc + 1, :] * cx
    for j in range(nsc):
        y = y + scw_ref[j:j + 1, :] * scbuf_ref[:, j * w:(j + 1) * w]
    ysc_ref[...] = (scb_ref[...] * y).astype(ysc_ref.dtype)
    for j in range(nsc - 1):
        scbuf_out_ref[:, j * w:(j + 1) * w] = scbuf_ref[:, (j + 1) * w:(j + 2) * w]
    scbuf_out_ref[:, (nsc - 1) * w:] = cx


def _mixer_sample(zr, h0, rg_buf, sc_buf, rg_conv_w, rg_conv_b, gate_w, rg_gate_b, rg_lambda,
                  sc_conv_w, layer):
    nb, w = h0.shape
    nrg, nsc = rg_buf.shape[1], sc_buf.shape[1]
    bw = w // LRU_BLOCKS

    def zcol(cidx):
        return pl.BlockSpec((nb, w), lambda i: (0, cidx))

    def full(n):
        return pl.BlockSpec((nb, n * w), lambda i: (0, 0))

    def per_layer(n):
        return pl.BlockSpec((None, n, w), lambda i: (layer, 0, 0))

    outs = pl.pallas_call(
        functools.partial(_mixer_step_kernel, nrg=nrg, nsc=nsc),
        grid=(1,),
        in_specs=[zcol(0), zcol(1), zcol(2), zcol(3), zcol(4),
                  full(1), full(nrg), full(nsc),
                  per_layer(nrg + 1), per_layer(1),
                  pl.BlockSpec((None, 2, LRU_BLOCKS, bw, bw), lambda i: (layer, 0, 0, 0, 0)),
                  per_layer(2), per_layer(1), per_layer(nsc + 1)],
        out_specs=[full(1), full(1), full(1), full(nrg), full(nsc)],
        out_shape=[jax.ShapeDtypeStruct((nb, w), BF16),
                   jax.ShapeDtypeStruct((nb, w), BF16),
                   jax.ShapeDtypeStruct((nb, w), F32),
                   jax.ShapeDtypeStruct((nb, nrg * w), F32),
                   jax.ShapeDtypeStruct((nb, nsc * w), F32)],
        compiler_params=_cparams("arbitrary"),
        name="mixer_sample",
    )(zr, zr, zr, zr, zr, h0, rg_buf.reshape(nb, nrg * w), sc_buf.reshape(nb, nsc * w),
      rg_conv_w, rg_conv_b, gate_w, rg_gate_b, rg_lambda, sc_conv_w)
    y_rg, y_sc, h, rgb, scb = outs
    return y_rg, y_sc, h, rgb.reshape(nb, nrg, w), scb.reshape(nb, nsc, w)


def _merge_kernel(x_ref, o_ref, yrg_ref, ysc_ref, gz0_ref, gz1_ref, gz2_ref, g_ref,
                  wa_ref, wr_ref, ws_ref, wo_ref, out_ref):
    m = _sigmoid(gz0_ref[...]) * _dot(o_ref[...], wa_ref[...])
    m = m + _sigmoid(gz1_ref[...]) * _dot(yrg_ref[...], wr_ref[...])
    m = m + _sigmoid(gz2_ref[...]) * _dot(ysc_ref[...], ws_ref[...])
    y = _dot(m.astype(BF16), wo_ref[...])
    out_ref[...] = x_ref[...] + _rms(y, g_ref[3:4, :])


def _merge(x, o, y_rg, y_sc, zr, norm_g, w_a, w_r, w_s, w_o, layer, tm, gz_col):
    rows, d = x.shape

    def rowblk(cidx=0):
        return pl.BlockSpec((tm, d), lambda i: (i, cidx))

    def wspec(arr):
        return _resident((None,) + arr.shape[1:], lambda i: (layer, 0, 0))

    return pl.pallas_call(
        _merge_kernel,
        grid=(rows // tm,),
        in_specs=[rowblk(), rowblk(), rowblk(), rowblk(),
                  rowblk(gz_col), rowblk(gz_col + 1), rowblk(gz_col + 2),
                  _resident((None, 6, d), lambda i: (layer, 0, 0)),
                  wspec(w_a), wspec(w_r), wspec(w_s), wspec(w_o)],
        out_specs=rowblk(),
        out_shape=jax.ShapeDtypeStruct((rows, d), F32),
        compiler_params=_cparams("parallel"),
        name="merge",
    )(x, o, y_rg, y_sc, zr, zr, zr, norm_g, w_a, w_r, w_s, w_o)


def _pick_tile(n, pref):
    t = min(n, pref)
    while n % t:
        t //= 2
    return t


def kernel(x_prompt, x_sample, cache_k, cache_v, page_table, state_rglru_h, state_rglru_conv, state_sconv, norm_g, w_ffn_up, w_ffn_down, w_in, lambda_qk, subln_g, rg_conv_w, rg_conv_b, rg_gate_w, rg_gate_b, rg_lambda, sc_conv_w, w_branch_attn, w_branch_rg, w_branch_sc, w_out):
    batch, seq, d = x_prompt.shape
    nb, dec_seq, _ = x_sample.shape
    assert dec_seq == 1, "the sample group carries one new token per sequence"
    depth = w_in.shape[0]
    n_pool, page = cache_k.shape[1], cache_k.shape[2]
    head_dim = cache_k.shape[-1]
    hd2 = 2 * head_dim
    qw = N_HEADS * hd2
    kw = N_KV_HEADS * hd2
    w = state_rglru_h.shape[-1]
    assert qw == d and w == d and qw + 2 * kw == 2 * d
    scale = head_dim ** -0.5
    assert math.log2(scale) == round(math.log2(scale)), "q pre-scaling must be exact in bf16"
    n_pages = page_table.shape[1]
    nrg, nsc = state_rglru_conv.shape[2], state_sconv.shape[2]

    w_up_b = w_ffn_up.astype(BF16)
    w_dn_b = w_ffn_down.astype(BF16)
    w_in_b = w_in.astype(BF16)
    w_a_b = w_branch_attn.astype(BF16)
    w_r_b = w_branch_rg.astype(BF16)
    w_s_b = w_branch_sc.astype(BF16)
    w_o_b = w_out.astype(BF16)
    gate_w_b = rg_gate_w.astype(BF16)

    slopes_np = np.float32(2.0) ** (-8.0 * np.arange(1, N_HEADS + 1, dtype=np.float32) / N_HEADS)
    assert all(math.frexp(float(s))[0] == 0.5 for s in slopes_np), "slopes must be bf16-exact"
    slopes = jnp.asarray(slopes_np, F32)
    cache_kt = jnp.transpose(cache_k, (0, 1, 3, 4, 5, 2)).reshape(depth, n_pool, kw, page)
    cache_v2 = cache_v.reshape(depth, n_pool, page * N_KV_HEADS, hd2)
    rg_conv_b3 = rg_conv_b.reshape(depth, 1, w)
    rg_lambda3 = rg_lambda.reshape(depth, 1, w)
    subln_g3 = subln_g.reshape(depth, 1, hd2)
    eye_kv = jnp.eye(N_KV_HEADS, dtype=BF16)
    eye_m = jnp.eye(2, dtype=BF16)

    tm = _pick_tile(batch * seq, 512)
    tq = _pick_tile(seq, 256)
    tt = _pick_tile(seq, 256)
    pps = _pick_tile(n_pages, 16)
    gz_col = 5

    xp = x_prompt.reshape(batch * seq, d)
    xs = x_sample.reshape(nb, d)
    zeros_h = jnp.zeros((batch, 1, w), F32)
    zeros_rg = jnp.zeros((batch, nrg, w), F32)
    zeros_sc = jnp.zeros((batch, nsc, w), F32)

    p_states, s_states = [], []
    for l in range(depth):
        lam_init = 0.8 - 0.6 * math.exp(-0.3 * l)
        dense = dict(norm_g=norm_g, layer=l)

        xp = _ffn(xp, norm_g, w_up_b, w_dn_b, l, 0, tm)
        q, k, v, zr = _inproj(xp, norm_g, w_in_b, l, tm, qw, kw, scale)
        o = _attn_prompt(q, k, v, slopes, lambda_qk, subln_g3, l, batch, seq, tq, lam_init)
        y_rg, y_sc, h_p, rgb_p, scb_p = _mixer_prompt(
            zr, zeros_h, zeros_rg, zeros_sc, rg_conv_w, rg_conv_b3, gate_w_b, rg_gate_b,
            rg_lambda3, sc_conv_w, l, batch, seq, tt)
        xp = _merge(xp, o, y_rg, y_sc, zr, norm_g, w_a_b, w_r_b, w_s_b, w_o_b, l, tm, gz_col)
        xp = _ffn(xp, norm_g, w_up_b, w_dn_b, l, 1, tm)
        p_states.append((k.reshape(batch, seq, N_KV_HEADS, 2, head_dim),
                         v.reshape(batch, seq, N_KV_HEADS, hd2),
                         h_p.reshape(batch, w), rgb_p, scb_p))

        xs = _ffn(xs, norm_g, w_up_b, w_dn_b, l, 0, nb)
        q, k, v, zr = _inproj(xs, norm_g, w_in_b, l, nb, qw, kw, scale)
        q5 = q.reshape(nb, N_KV_HEADS, HEAD_GROUP, 2, head_dim)
        qt = jnp.einsum('bkgmd,kK,mM->bkgmKMd', q5, eye_kv, eye_m).reshape(nb, 2 * N_HEADS, kw)
        o = _attn_paged(qt, k.reshape(nb, 1, kw), v.reshape(nb, 1, kw), cache_kt, cache_v2,
                        page_table, slopes, lambda_qk, subln_g3, l, pps, lam_init)
        o = o.reshape(nb, qw)
        y_rg, y_sc, h_s, rgb_s, scb_s = _mixer_sample(
            zr, state_rglru_h[l], state_rglru_conv[l], state_sconv[l], rg_conv_w, rg_conv_b3,
            gate_w_b, rg_gate_b, rg_lambda3, sc_conv_w, l)
        xs = _merge(xs, o, y_rg, y_sc, zr, norm_g, w_a_b, w_r_b, w_s_b, w_o_b, l, nb, gz_col)
        xs = _ffn(xs, norm_g, w_up_b, w_dn_b, l, 1, nb)
        s_states.append((k.reshape(nb, 1, N_KV_HEADS, 2, head_dim),
                         v.reshape(nb, 1, N_KV_HEADS, hd2), h_s, rgb_s, scb_s))

    k_p, v_p, h_p, rgc_p, sc_p = [jnp.stack(s, axis=0) for s in zip(*p_states)]
    k_s, v_s, h_s, rgc_s, sc_s = [jnp.stack(s, axis=0) for s in zip(*s_states)]
    return (xp.reshape(batch, seq, d), xs.reshape(nb, 1, d), k_p, v_p, h_p, rgc_p, sc_p,
            k_s, v_s, h_s, rgc_s, sc_s)
```

```python
import functools
import math

import numpy as np
import jax
import jax.numpy as jnp
from jax import lax
from jax.experimental import pallas as pl
from jax.experimental.pallas import tpu as pltpu

F32 = jnp.float32
BF16 = jnp.bfloat16

NORM_EPS = 1e-6
LRU_C = 8.0
N_HEADS = 8
N_KV_HEADS = 4
HEAD_GROUP = N_HEADS // N_KV_HEADS
LRU_BLOCKS = 8
N_BRANCHES = 3
MASK_VALUE = -1e30
LOG2_E = math.log2(math.e)
VMEM_LIMIT_BYTES = 56 * 1024 * 1024


def _cparams(*sem):
    return pltpu.CompilerParams(dimension_semantics=sem, vmem_limit_bytes=VMEM_LIMIT_BYTES)


def _rms(x, g):
    return x * lax.rsqrt(jnp.mean(x * x, axis=-1, keepdims=True) + NORM_EPS) * g


def _dot(a, b):
    return jnp.dot(a, b, preferred_element_type=F32)


def _dot_nt(a, b):
    return lax.dot_general(a, b, (((1,), (1,)), ((), ())), preferred_element_type=F32)


def _sigmoid(x):
    return 0.5 * jnp.tanh(0.5 * x) + 0.5


def _gelu_tanh(x):
    c = math.sqrt(2.0 / math.pi)
    return 0.5 * x * (1.0 + jnp.tanh(c * (x + 0.044715 * (x * x * x))))


def _resident(shape, index_map):
    return pl.BlockSpec(shape, index_map, pipeline_mode=pl.Buffered(1))


def _ffn_kernel(x_ref, g_ref, wup_ref, wdn_ref, o_ref, *, d_ff, chunks, g_pre, g_post):
    x = x_ref[...]
    h = _rms(x, g_ref[g_pre:g_pre + 1, :]).astype(BF16)
    acc = None
    for c0, cw in chunks:
        gate = _dot(h, wup_ref[:, c0:c0 + cw])
        up = _dot(h, wup_ref[:, d_ff + c0:d_ff + c0 + cw])
        act = (gate * _sigmoid(gate) * up).astype(BF16)
        part = _dot(act, wdn_ref[c0:c0 + cw, :])
        acc = part if acc is None else acc + part
    o_ref[...] = x + 0.5 * _rms(acc, g_ref[g_post:g_post + 1, :])


def _ffn(x, norm_g, w_up, w_dn, layer, which, tm):
    rows, d = x.shape
    d_ff = w_dn.shape[2]
    chunk = 1024
    chunks = tuple((c0, min(chunk, d_ff - c0)) for c0 in range(0, d_ff, chunk))
    kern = functools.partial(_ffn_kernel, d_ff=d_ff, chunks=chunks,
                             g_pre=0 if which == 0 else 4, g_post=1 if which == 0 else 5)
    return pl.pallas_call(
        kern,
        grid=(rows // tm,),
        in_specs=[
            pl.BlockSpec((tm, d), lambda i: (i, 0)),
            _resident((None, 6, d), lambda i: (layer, 0, 0)),
            _resident((None, None, d, 2 * d_ff), lambda i: (layer, which, 0, 0)),
            _resident((None, None, d_ff, d), lambda i: (layer, which, 0, 0)),
        ],
        out_specs=pl.BlockSpec((tm, d), lambda i: (i, 0)),
        out_shape=jax.ShapeDtypeStruct((rows, d), F32),
        compiler_params=_cparams("parallel"),
        name=f"ffn{which}",
    )(x, norm_g, w_up, w_dn)


def _qkv_kernel(x_ref, g_ref, w_ref, q_ref, k_ref, v_ref, *, qw, kw, scale):
    h = _rms(x_ref[...], g_ref[2:3, :]).astype(BF16)
    z = _dot(h, w_ref[...])
    q_ref[...] = (z[:, :qw] * scale).astype(BF16)
    k_ref[...] = z[:, qw:qw + kw]
    v_ref[...] = z[:, qw + kw:]


def _zrest_kernel(x_ref, g_ref, w_ref, z_ref):
    h = _rms(x_ref[...], g_ref[2:3, :]).astype(BF16)
    z_ref[...] = _dot(h, w_ref[...])


def _inproj(x, norm_g, w_in, layer, tm, qw, kw, scale):
    rows, d = x.shape
    in_w = w_in.shape[2]
    cw = qw + 2 * kw
    n_rest = (in_w - cw) // cw
    assert cw * (n_rest + 1) == in_w
    q, k, v = pl.pallas_call(
        functools.partial(_qkv_kernel, qw=qw, kw=kw, scale=scale),
        grid=(rows // tm,),
        in_specs=[
            pl.BlockSpec((tm, d), lambda i: (i, 0)),
            _resident((None, 6, d), lambda i: (layer, 0, 0)),
            _resident((None, d, cw), lambda i: (layer, 0, 0)),
        ],
        out_specs=[
            pl.BlockSpec((tm, qw), lambda i: (i, 0)),
            pl.BlockSpec((tm, kw), lambda i: (i, 0)),
            pl.BlockSpec((tm, kw), lambda i: (i, 0)),
        ],
        out_shape=[
            jax.ShapeDtypeStruct((rows, qw), BF16),
            jax.ShapeDtypeStruct((rows, kw), F32),
            jax.ShapeDtypeStruct((rows, kw), F32),
        ],
        compiler_params=_cparams("parallel"),
        name="inproj_qkv",
    )(x, norm_g, w_in)
    tr = tm
    zr = pl.pallas_call(
        _zrest_kernel,
        grid=(n_rest, rows // tr),
        in_specs=[
            pl.BlockSpec((tr, d), lambda j, i: (i, 0)),
            _resident((None, 6, d), lambda j, i: (layer, 0, 0)),
            pl.BlockSpec((None, d, cw), lambda j, i: (layer, 0, j + 1)),
        ],
        out_specs=pl.BlockSpec((tr, cw), lambda j, i: (i, j)),
        out_shape=jax.ShapeDtypeStruct((rows, in_w - cw), F32),
        compiler_params=_cparams("parallel", "parallel"),
        name="inproj_rest",
    )(x, norm_g, w_in)
    return q, k, v, zr


def _lambda_full(lqk_ref, lam_init):
    s01 = jnp.sum(lqk_ref[0:1, :] * lqk_ref[1:2, :], axis=-1, keepdims=True)
    s23 = jnp.sum(lqk_ref[2:3, :] * lqk_ref[3:4, :], axis=-1, keepdims=True)
    return jnp.exp(s01) - jnp.exp(s23) + lam_init


def _subln(o, sg, lam_init):
    return _rms(o, sg) * (1.0 - lam_init)


def _loop_unrolled(n, fn, unroll):
    def body(jj, carry):
        for u in range(unroll):
            fn(unroll * jj + u)
        return carry

    lax.fori_loop(0, n // unroll, body, 0)
    if unroll == 4:
        base = (n // 4) * 4

        @pl.when(n % 4 >= 2)
        def _():
            fn(base)
            fn(base + 1)

    @pl.when(n % 2 == 1)
    def _():
        fn(n - 1)


def _attn_kernel(slopes_ref, q_ref, k_ref, v_ref, lqk_ref, sg_ref, o_ref,
                 kaug, vbf, s_scr, mx_scr, ls_scr, acc_scr, *, tq, lam_init):
    kvh = pl.program_id(1)
    seq, hd2 = k_ref.shape
    hd = hd2 // 2
    tk = tq
    nc = tk // 128
    nq = seq // tq
    lane = lax.broadcasted_iota(jnp.int32, (1, hd2), 1)

    pos = lax.broadcasted_iota(jnp.int32, (seq, 1), 0)
    lo = pos & 7
    hi = (pos - lo).astype(F32)
    lo = lo.astype(F32)
    k = k_ref[...]
    kaug[0] = jnp.where(lane < hd, k, jnp.where(lane == hd, hi,
                        jnp.where(lane == hd + 1, lo, 0.0))).astype(BF16)
    kaug[1] = jnp.where(lane >= hd, k, jnp.where(lane == 0, hi,
                        jnp.where(lane == 1, lo, 0.0))).astype(BF16)
    vbf[...] = v_ref[...].astype(BF16)

    row = lax.broadcasted_iota(jnp.int32, (2 * tq, 1), 0)
    slope = jnp.where(row < tq, slopes_ref[2 * kvh], slopes_ref[2 * kvh + 1])
    diagonal_visible = (lax.broadcasted_iota(jnp.int32, (1, tk), 1)
                        <= jnp.where(row < tq, row, row - tq))
    lam = _lambda_full(lqk_ref, lam_init)

    def q_tile(i):
        q = q_ref[pl.ds(pl.multiple_of(i * tq, tq), tq), :].astype(F32)
        qs = jnp.concatenate([q[:, :hd2], q[:, hd2:]], axis=0)
        return (jnp.where(lane < hd, qs, jnp.where(lane < hd + 2, slope, 0.0)).astype(BF16),
                jnp.where(lane >= hd, qs, jnp.where(lane < 2, slope, 0.0)).astype(BF16))

    def scores(qm, buf, j, on_diagonal):
        k0 = pl.multiple_of(j * tk, tk)
        for mm in range(2):
            s = _dot_nt(qm[mm], kaug[mm, pl.ds(k0, tk), :]) * LOG2_E
            if on_diagonal:
                s = jnp.where(diagonal_visible, s, MASK_VALUE)
            s_scr[buf, mm, j] = s
            mx = mx_scr[buf, mm]
            for c in range(nc):
                mx = jnp.maximum(mx, s[:, c * 128:(c + 1) * 128])
            mx_scr[buf, mm] = mx

    def weigh(buf, j):
        k0 = pl.multiple_of(j * tk, tk)
        vb = vbf[pl.ds(k0, tk), :]
        for mm in range(2):
            s = s_scr[buf, mm, j]
            mb = mx_scr[buf, mm]
            ps = [jnp.exp2(s[:, c * 128:(c + 1) * 128] - mb) for c in range(nc)]
            ls = ls_scr[mm]
            for c in range(nc):
                ls = ls + ps[c]
            ls_scr[mm] = ls
            p = jnp.concatenate(ps, axis=-1).astype(BF16)
            acc_scr[mm] = acc_scr[mm] + _dot(p, vb)

    def open_scores(buf):
        mx_scr[buf] = jnp.full(mx_scr.shape[1:], MASK_VALUE, F32)

    def close_scores(buf):
        for mm in range(2):
            mx_scr[buf, mm] = jnp.broadcast_to(
                jnp.max(mx_scr[buf, mm], axis=-1, keepdims=True), mx_scr.shape[2:])
        ls_scr[...] = jnp.zeros(ls_scr.shape, F32)
        acc_scr[...] = jnp.zeros(acc_scr.shape, F32)

    def emit(i):
        l0 = jnp.sum(ls_scr[0], axis=-1, keepdims=True)
        l1 = jnp.sum(ls_scr[1], axis=-1, keepdims=True)
        o = acc_scr[0] / l0 - lam * (acc_scr[1] / l1)
        o = _subln(o, sg_ref[...], lam_init).astype(o_ref.dtype)
        r0 = pl.multiple_of(i * tq, tq)
        o_ref[pl.ds(r0, tq), :hd2] = o[:tq]
        o_ref[pl.ds(r0, tq), hd2:] = o[tq:]

    open_scores(0)
    scores(q_tile(0), 0, 0, True)

    def q_step(i, carry):
        cur = i % 2
        nxt = 1 - cur
        close_scores(cur)
        q_next = q_tile(i + 1)
        open_scores(nxt)

        def both(j):
            weigh(cur, j)
            scores(q_next, nxt, j, False)

        _loop_unrolled(i + 1, both, 2)
        scores(q_next, nxt, i + 1, True)
        emit(i)
        return carry

    lax.fori_loop(0, nq - 1, q_step, 0)
    last = (nq - 1) % 2
    close_scores(last)
    _loop_unrolled(nq, lambda j: weigh(last, j), 4)
    emit(nq - 1)


def _attn_prompt(q, k, v, slopes, lambda_qk, subln_g, layer, batch, seq, tq, lam_init):
    rows, qw = q.shape
    hd2 = k.shape[1] // N_KV_HEADS
    nq = seq // tq
    assert tq % 128 == 0 and seq % tq == 0
    kern = functools.partial(_attn_kernel, tq=tq, lam_init=lam_init)
    return pl.pallas_call(
        kern,
        grid_spec=pltpu.PrefetchScalarGridSpec(
            num_scalar_prefetch=1,
            grid=(batch, N_KV_HEADS),
            in_specs=[
                pl.BlockSpec((seq, HEAD_GROUP * hd2), lambda b, h, s: (b, h)),
                pl.BlockSpec((seq, hd2), lambda b, h, s: (b, h)),
                pl.BlockSpec((seq, hd2), lambda b, h, s: (b, h)),
                pl.BlockSpec((None, 4, hd2 // 2), lambda b, h, s: (layer, 0, 0)),
                pl.BlockSpec((None, 1, hd2), lambda b, h, s: (layer, 0, 0)),
            ],
            out_specs=pl.BlockSpec((seq, HEAD_GROUP * hd2), lambda b, h, s: (b, h)),
            scratch_shapes=[
                pltpu.VMEM((2, seq, hd2), BF16),
                pltpu.VMEM((seq, hd2), BF16),
                pltpu.VMEM((2, 2, nq, 2 * tq, tq), F32),
                pltpu.VMEM((2, 2, 2 * tq, 128), F32),
                pltpu.VMEM((2, 2 * tq, 128), F32),
                pltpu.VMEM((2, 2 * tq, hd2), F32),
            ],
        ),
        out_shape=jax.ShapeDtypeStruct((rows, qw), BF16),
        compiler_params=_cparams("parallel", "parallel"),
        name="attn_prompt",
    )(slopes, q, k, v, lambda_qk, subln_g)


def _paged_kernel(pt_ref, slopes_ref, qt_ref, ks_ref, vs_ref, lqk_ref, sg_ref, *refs,
                  pages_per_step, page, past, lam_init):
    del pt_ref
    pp = pages_per_step
    kt_refs = refs[:pp]
    v_refs = refs[pp:2 * pp]
    o_ref = refs[2 * pp]
    m_ref, l_ref, acc_ref = refs[2 * pp + 1:]
    c = pl.program_id(1)
    nrow = qt_ref.shape[0]
    hd2 = sg_ref.shape[1]
    qt = qt_ref[...]
    row = lax.broadcasted_iota(jnp.int32, (nrow, 1), 0)
    row_kv = row // (2 * HEAD_GROUP)
    head = row // 2
    slope = jnp.zeros((nrow, 1), F32)
    for h in range(N_HEADS):
        slope = jnp.where(head == h, slopes_ref[h], slope)

    @pl.when(c == 0)
    def _():
        ks = ks_ref[...].astype(BF16).astype(F32)
        m_ref[...] = jnp.sum(qt.astype(F32) * ks, axis=-1, keepdims=True)
        l_ref[...] = jnp.ones(l_ref.shape, F32)
        vs = vs_ref[...].astype(BF16).astype(F32)
        a = jnp.zeros(acc_ref.shape, F32)
        for kv in range(N_KV_HEADS):
            a = jnp.where(row_kv == kv, vs[:, kv * hd2:(kv + 1) * hd2], a)
        acc_ref[...] = a

    s_parts = []
    for r in range(pp):
        kpos = (c * pp + r) * page + lax.broadcasted_iota(jnp.int32, (1, page), 1)
        dist = (past - kpos).astype(F32)
        s_parts.append(_dot(qt, kt_refs[r][...].astype(BF16)) - slope * dist)
    s = jnp.concatenate(s_parts, axis=-1)
    m_old = m_ref[...]
    m_new = jnp.maximum(m_old, jnp.max(s, axis=-1, keepdims=True))
    p = jnp.exp(s - m_new)
    corr = jnp.exp(m_old - m_new)
    l_ref[...] = corr * l_ref[...] + jnp.sum(p, axis=-1, keepdims=True)
    pv = None
    for kv in range(N_KV_HEADS):
        pk = jnp.where(row_kv == kv, p, 0.0).astype(BF16)
        for r in range(pp):
            v_kv = v_refs[r][pl.ds(kv, page, stride=N_KV_HEADS), :].astype(BF16)
            part = _dot(pk[:, r * page:(r + 1) * page], v_kv)
            pv = part if pv is None else pv + part
    acc_ref[...] = corr * acc_ref[...] + pv
    m_ref[...] = m_new

    @pl.when(c == pl.num_programs(1) - 1)
    def _():
        lam = _lambda_full(lqk_ref, lam_init)
        acc_ref[...] = acc_ref[...] / l_ref[...]
        a0 = acc_ref[pl.ds(0, N_HEADS, stride=2), :]
        a1 = acc_ref[pl.ds(1, N_HEADS, stride=2), :]
        o_ref[...] = _subln(a0 - lam * a1, sg_ref[...], lam_init).astype(o_ref.dtype)


def _attn_paged(qt, k_new, v_new, cache_kt, cache_v2, page_table, slopes, lambda_qk, subln_g,
                layer, pages_per_step, lam_init):
    nb, nrow, kvw = qt.shape
    page = cache_kt.shape[3]
    n_pages = page_table.shape[1]
    pp = pages_per_step
    hd2 = kvw // N_KV_HEADS
    assert cache_kt.shape[2] == kvw and cache_v2.shape[2:] == (page * N_KV_HEADS, hd2)
    kern = functools.partial(_paged_kernel, pages_per_step=pp, page=page,
                             past=n_pages * page, lam_init=lam_init)

    def page_spec(arr, r):
        return pl.BlockSpec((None, None) + arr.shape[2:],
                            lambda b, c, pt, s: (layer, pt[b, c * pp + r], 0, 0))

    return pl.pallas_call(
        kern,
        grid_spec=pltpu.PrefetchScalarGridSpec(
            num_scalar_prefetch=2,
            grid=(nb, n_pages // pp),
            in_specs=[
                pl.BlockSpec((None, nrow, kvw), lambda b, c, pt, s: (b, 0, 0)),
                pl.BlockSpec((None, 1, kvw), lambda b, c, pt, s: (b, 0, 0)),
                pl.BlockSpec((None, 1, kvw), lambda b, c, pt, s: (b, 0, 0)),
                pl.BlockSpec((None, 4, hd2 // 2), lambda b, c, pt, s: (layer, 0, 0)),
                pl.BlockSpec((None, 1, hd2), lambda b, c, pt, s: (layer, 0, 0)),
            ] + [page_spec(cache_kt, r) for r in range(pp)]
              + [page_spec(cache_v2, r) for r in range(pp)],
            out_specs=pl.BlockSpec((None, N_HEADS, hd2), lambda b, c, pt, s: (b, 0, 0)),
            scratch_shapes=[
                pltpu.VMEM((nrow, 1), F32),
                pltpu.VMEM((nrow, 1), F32),
                pltpu.VMEM((nrow, hd2), F32),
            ],
        ),
        out_shape=jax.ShapeDtypeStruct((nb, N_HEADS, hd2), BF16),
        compiler_params=_cparams("parallel", "arbitrary"),
        name="attn_paged",
    )(page_table, slopes, qt, k_new, v_new, lambda_qk, subln_g,
      *([cache_kt] * pp), *([cache_v2] * pp))


def _lru_gates(xc, gw_ref, gb_ref, lam_ref):
    w = xc.shape[1]
    bw = w // LRU_BLOCKS
    g0, g1 = [], []
    for n in range(LRU_BLOCKS):
        xb = xc[:, n * bw:(n + 1) * bw].astype(BF16)
        g0.append(_dot(xb, gw_ref[0, n]))
        g1.append(_dot(xb, gw_ref[1, n]))
    r = _sigmoid(jnp.concatenate(g0, axis=-1) + gb_ref[0:1, :])
    i = _sigmoid(jnp.concatenate(g1, axis=-1) + gb_ref[1:2, :])
    neg_lam = -lam_ref[...]
    softplus = jnp.maximum(neg_lam, 0.0) + jnp.log1p(jnp.exp(-jnp.abs(neg_lam)))
    a = jnp.exp2(r * ((-LRU_C * LOG2_E) * softplus))
    d = 1.0 - a * a
    root = jnp.where(d > 0.0, d * lax.rsqrt(d), 0.0)
    u = root * (i * xc)
    return a, u


def _grouped(x):
    return x.reshape(x.shape[0] // 8, 8, x.shape[1])


def _shift_rows(x3, prev8, s, sub):
    rolled = pltpu.roll(jnp.concatenate([prev8[None], x3], axis=0), s, 1)
    return jnp.where(sub >= s, rolled[1:], rolled[:-1])


def _scan_groups(a3, u3, h_in):
    sub = lax.broadcasted_iota(jnp.int32, (1, 8, 1), 1)
    for s in (1, 2, 4):
        keep = sub >= s
        a_sh = jnp.where(keep, pltpu.roll(a3, s, 1), 1.0)
        u_sh = jnp.where(keep, pltpu.roll(u3, s, 1), 0.0)
        u3 = a3 * u_sh + u3
        a3 = a3 * a_sh
    out = []
    h_prev = h_in
    for g in range(a3.shape[0]):
        hg = u3[g] + a3[g] * h_prev
        out.append(hg)
        h_prev = hg[7:8, :]
    return jnp.concatenate(out, axis=0)


def _mixer_kernel(rgx_ref, rgg_ref, scb_ref, scc_ref, scx_ref, h0_ref, rgbuf_ref, scbuf_ref,
                  cw_ref, cb_ref, gw_ref, gb_ref, lam_ref, scw_ref,
                  yrg_ref, ysc_ref, hout_ref, rgbuf_out_ref, scbuf_out_ref,
                  prev_rg, prev_sc, hcarry, *, tt):
    t = pl.program_id(1)
    nrg = rgbuf_ref.shape[0]
    nsc = scbuf_ref.shape[0]
    w = rgx_ref.shape[1]
    sub = lax.broadcasted_iota(jnp.int32, (1, 8, 1), 1)

    @pl.when(t == 0)
    def _():
        prev_rg[...] = jnp.zeros(prev_rg.shape, F32)
        prev_sc[...] = jnp.zeros(prev_sc.shape, F32)
        prev_rg[8 - nrg:8, :] = rgbuf_ref[...]
        prev_sc[8 - nsc:8, :] = scbuf_ref[...]
        hcarry[...] = h0_ref[...]

    x = rgx_ref[...]
    x3 = _grouped(x)
    prev = prev_rg[...]
    xc3 = cb_ref[...] + cw_ref[nrg:nrg + 1, :] * x3
    for s in range(1, nrg + 1):
        xc3 = xc3 + cw_ref[nrg - s:nrg - s + 1, :] * _shift_rows(x3, prev, s, sub)
    prev_rg[...] = x[tt - 8:tt, :]
    a, u = _lru_gates(xc3.reshape(tt, w), gw_ref, gb_ref, lam_ref)
    h = _scan_groups(_grouped(a), _grouped(u), hcarry[...])
    hcarry[...] = h[tt - 1:tt, :]
    yrg_ref[...] = (h * _gelu_tanh(rgg_ref[...])).astype(yrg_ref.dtype)

    cx = scc_ref[...] * scx_ref[...]
    cx3 = _grouped(cx)
    prev = prev_sc[...]
    y3 = scw_ref[nsc:nsc + 1, :] * cx3
    for s in range(1, nsc + 1):
        y3 = y3 + scw_ref[nsc - s:nsc - s + 1, :] * _shift_rows(cx3, prev, s, sub)
    prev_sc[...] = cx[tt - 8:tt, :]
    ysc_ref[...] = (scb_ref[...] * y3.reshape(tt, w)).astype(ysc_ref.dtype)

    @pl.when(t == pl.num_programs(1) - 1)
    def _():
        hout_ref[...] = hcarry[...]
        rgbuf_out_ref[...] = prev_rg[8 - nrg:8, :]
        scbuf_out_ref[...] = prev_sc[8 - nsc:8, :]


def _mixer_prompt(zr, h0, rg_buf, sc_buf, rg_conv_w, rg_conv_b, gate_w, rg_gate_b, rg_lambda,
                  sc_conv_w, layer, batch, seq, tt):
    rows = zr.shape[0]
    w = h0.shape[-1]
    nt = seq // tt
    nrg, nsc = rg_buf.shape[1], sc_buf.shape[1]

    def zcol(cidx):
        return pl.BlockSpec((tt, w), lambda b, t: (b * nt + t, cidx))

    def per_batch(n):
        return pl.BlockSpec((None, n, w), lambda b, t: (b, 0, 0))

    def per_layer(n):
        return pl.BlockSpec((None, n, w), lambda b, t: (layer, 0, 0))

    bw = w // LRU_BLOCKS
    return pl.pallas_call(
        functools.partial(_mixer_kernel, tt=tt),
        grid=(batch, nt),
        in_specs=[zcol(0), zcol(1), zcol(2), zcol(3), zcol(4),
                  per_batch(1), per_batch(nrg), per_batch(nsc),
                  per_layer(nrg + 1), per_layer(1),
                  pl.BlockSpec((None, 2, LRU_BLOCKS, bw, bw), lambda b, t: (layer, 0, 0, 0, 0)),
                  per_layer(2), per_layer(1), per_layer(nsc + 1)],
        out_specs=[pl.BlockSpec((tt, w), lambda b, t: (b * nt + t, 0)),
                   pl.BlockSpec((tt, w), lambda b, t: (b * nt + t, 0)),
                   per_batch(1), per_batch(nrg), per_batch(nsc)],
        out_shape=[jax.ShapeDtypeStruct((rows, w), BF16),
                   jax.ShapeDtypeStruct((rows, w), BF16),
                   jax.ShapeDtypeStruct((batch, 1, w), F32),
                   jax.ShapeDtypeStruct((batch, nrg, w), F32),
                   jax.ShapeDtypeStruct((batch, nsc, w), F32)],
        scratch_shapes=[pltpu.VMEM((8, w), F32), pltpu.VMEM((8, w), F32),
                        pltpu.VMEM((1, w), F32)],
        compiler_params=_cparams("parallel", "arbitrary"),
        name="mixer_prompt",
    )(zr, zr, zr, zr, zr, h0, rg_buf, sc_buf, rg_conv_w, rg_conv_b, gate_w, rg_gate_b,
      rg_lambda, sc_conv_w)


def _mixer_step_kernel(rgx_ref, rgg_ref, scb_ref, scc_ref, scx_ref, h0_ref, rgbuf_ref, scbuf_ref,
                       cw_ref, cb_ref, gw_ref, gb_ref, lam_ref, scw_ref,
                       yrg_ref, ysc_ref, hout_ref, rgbuf_out_ref, scbuf_out_ref, *, nrg, nsc):
    w = h0_ref.shape[1]
    x = rgx_ref[...]
    xc = cb_ref[...] + cw_ref[nrg:nrg + 1, :] * x
    for j in range(nrg):
        xc = xc + cw_ref[j:j + 1, :] * rgbuf_ref[:, j * w:(j + 1) * w]
    a, u = _lru_gates(xc, gw_ref, gb_ref, lam_ref)
    h = a * h0_ref[...] + u
    hout_ref[...] = h
    yrg_ref[...] = (h * _gelu_tanh(rgg_ref[...])).astype(yrg_ref.dtype)
    for j in range(nrg - 1):
        rgbuf_out_ref[:, j * w:(j + 1) * w] = rgbuf_ref[:, (j + 1) * w:(j + 2) * w]
    rgbuf_out_ref[:, (nrg - 1) * w:] = x

    cx = scc_ref[...] * scx_ref[...]
    y = scw_ref[nsc:nsc + 1, :] * cx
    for j in range(nsc):
        y = y + scw_ref[j:j + 1, :] * scbuf_ref[:, j * w:(j + 1) * w]
    ysc_ref[...] = (scb_ref[...] * y).astype(ysc_ref.dtype)
    for j in range(nsc - 1):
        scbuf_out_ref[:, j * w:(j + 1) * w] = scbuf_ref[:, (j + 1) * w:(j + 2) * w]
    scbuf_out_ref[:, (nsc - 1) * w:] = cx


def _mixer_sample(zr, h0, rg_buf, sc_buf, rg_conv_w, rg_conv_b, gate_w, rg_gate_b, rg_lambda,
                  sc_conv_w, layer):
    nb, w = h0.shape
    nrg, nsc = rg_buf.shape[1], sc_buf.shape[1]
    bw = w // LRU_BLOCKS

    def zcol(cidx):
        return pl.BlockSpec((nb, w), lambda i: (0, cidx))

    def full(n):
        return pl.BlockSpec((nb, n * w), lambda i: (0, 0))

    def per_layer(n):
        return pl.BlockSpec((None, n, w), lambda i: (layer, 0, 0))

    outs = pl.pallas_call(
        functools.partial(_mixer_step_kernel, nrg=nrg, nsc=nsc),
        grid=(1,),
        in_specs=[zcol(0), zcol(1), zcol(2), zcol(3), zcol(4),
                  full(1), full(nrg), full(nsc),
                  per_layer(nrg + 1), per_layer(1),
                  pl.BlockSpec((None, 2, LRU_BLOCKS, bw, bw), lambda i: (layer, 0, 0, 0, 0)),
                  per_layer(2), per_layer(1), per_layer(nsc + 1)],
        out_specs=[full(1), full(1), full(1), full(nrg), full(nsc)],
        out_shape=[jax.ShapeDtypeStruct((nb, w), BF16),
                   jax.ShapeDtypeStruct((nb, w), BF16),
                   jax.ShapeDtypeStruct((nb, w), F32),
                   jax.ShapeDtypeStruct((nb, nrg * w), F32),
                   jax.ShapeDtypeStruct((nb, nsc * w), F32)],
        compiler_params=_cparams("arbitrary"),
        name="mixer_sample",
    )(zr, zr, zr, zr, zr, h0, rg_buf.reshape(nb, nrg * w), sc_buf.reshape(nb, nsc * w),
      rg_conv_w, rg_conv_b, gate_w, rg_gate_b, rg_lambda, sc_conv_w)
    y_rg, y_sc, h, rgb, scb = outs
    return y_rg, y_sc, h, rgb.reshape(nb, nrg, w), scb.reshape(nb, nsc, w)


def _merge_kernel(x_ref, o_ref, yrg_ref, ysc_ref, gz0_ref, gz1_ref, gz2_ref, g_ref,
                  wa_ref, wr_ref, ws_ref, wo_ref, out_ref):
    m = _sigmoid(gz0_ref[...]) * _dot(o_ref[...], wa_ref[...])
    m = m + _sigmoid(gz1_ref[...]) * _dot(yrg_ref[...], wr_ref[...])
    m = m + _sigmoid(gz2_ref[...]) * _dot(ysc_ref[...], ws_ref[...])
    y = _dot(m.astype(BF16), wo_ref[...])
    out_ref[...] = x_ref[...] + _rms(y, g_ref[3:4, :])


def _merge(x, o, y_rg, y_sc, zr, norm_g, w_a, w_r, w_s, w_o, layer, tm, gz_col):
    rows, d = x.shape

    def rowblk(cidx=0):
        return pl.BlockSpec((tm, d), lambda i: (i, cidx))

    def wspec(arr):
        return _resident((None,) + arr.shape[1:], lambda i: (layer, 0, 0))

    return pl.pallas_call(
        _merge_kernel,
        grid=(rows // tm,),
        in_specs=[rowblk(), rowblk(), rowblk(), rowblk(),
                  rowblk(gz_col), rowblk(gz_col + 1), rowblk(gz_col + 2),
                  _resident((None, 6, d), lambda i: (layer, 0, 0)),
                  wspec(w_a), wspec(w_r), wspec(w_s), wspec(w_o)],
        out_specs=rowblk(),
        out_shape=jax.ShapeDtypeStruct((rows, d), F32),
        compiler_params=_cparams("parallel"),
        name="merge",
    )(x, o, y_rg, y_sc, zr, zr, zr, norm_g, w_a, w_r, w_s, w_o)


def _pick_tile(n, pref):
    t = min(n, pref)
    while n % t:
        t //= 2
    return t


def kernel(x_prompt, x_sample, cache_k, cache_v, page_table, state_rglru_h, state_rglru_conv, state_sconv, norm_g, w_ffn_up, w_ffn_down, w_in, lambda_qk, subln_g, rg_conv_w, rg_conv_b, rg_gate_w, rg_gate_b, rg_lambda, sc_conv_w, w_branch_attn, w_branch_rg, w_branch_sc, w_out):
    batch, seq, d = x_prompt.shape
    nb, dec_seq, _ = x_sample.shape
    assert dec_seq == 1, "the sample group carries one new token per sequence"
    depth = w_in.shape[0]
    n_pool, page = cache_k.shape[1], cache_k.shape[2]
    head_dim = cache_k.shape[-1]
    hd2 = 2 * head_dim
    qw = N_HEADS * hd2
    kw = N_KV_HEADS * hd2
    w = state_rglru_h.shape[-1]
    assert qw == d and w == d and qw + 2 * kw == 2 * d
    scale = head_dim ** -0.5
    assert math.log2(scale) == round(math.log2(scale)), "q pre-scaling must be exact in bf16"
    n_pages = page_table.shape[1]
    nrg, nsc = state_rglru_conv.shape[2], state_sconv.shape[2]

    w_up_b = w_ffn_up.astype(BF16)
    w_dn_b = w_ffn_down.astype(BF16)
    w_in_b = w_in.astype(BF16)
    w_a_b = w_branch_attn.astype(BF16)
    w_r_b = w_branch_rg.astype(BF16)
    w_s_b = w_branch_sc.astype(BF16)
    w_o_b = w_out.astype(BF16)
    gate_w_b = rg_gate_w.astype(BF16)

    slopes_np = np.float32(2.0) ** (-8.0 * np.arange(1, N_HEADS + 1, dtype=np.float32) / N_HEADS)
    assert all(math.frexp(float(s))[0] == 0.5 for s in slopes_np), "slopes must be bf16-exact"
    slopes = jnp.asarray(slopes_np, F32)
    cache_kt = jnp.transpose(cache_k, (0, 1, 3, 4, 5, 2)).reshape(depth, n_pool, kw, page)
    cache_v2 = cache_v.reshape(depth, n_pool, page * N_KV_HEADS, hd2)
    rg_conv_b3 = rg_conv_b.reshape(depth, 1, w)
    rg_lambda3 = rg_lambda.reshape(depth, 1, w)
    subln_g3 = subln_g.reshape(depth, 1, hd2)
    eye_kv = jnp.eye(N_KV_HEADS, dtype=BF16)
    eye_m = jnp.eye(2, dtype=BF16)

    tm = _pick_tile(batch * seq, 512)
    tm_wide = _pick_tile(batch * seq, 1024)
    tq = _pick_tile(seq, 256)
    tt = _pick_tile(seq, 512)
    pps = _pick_tile(n_pages, 16)
    gz_col = 5

    xp = x_prompt.reshape(batch * seq, d)
    xs = x_sample.reshape(nb, d)
    zeros_h = jnp.zeros((batch, 1, w), F32)
    zeros_rg = jnp.zeros((batch, nrg, w), F32)
    zeros_sc = jnp.zeros((batch, nsc, w), F32)

    p_states, s_states = [], []
    for l in range(depth):
        lam_init = 0.8 - 0.6 * math.exp(-0.3 * l)
        dense = dict(norm_g=norm_g, layer=l)

        xp = _ffn(xp, norm_g, w_up_b, w_dn_b, l, 0, tm_wide)
        q, k, v, zr = _inproj(xp, norm_g, w_in_b, l, tm_wide, qw, kw, scale)
        o = _attn_prompt(q, k, v, slopes, lambda_qk, subln_g3, l, batch, seq, tq, lam_init)
        y_rg, y_sc, h_p, rgb_p, scb_p = _mixer_prompt(
            zr, zeros_h, zeros_rg, zeros_sc, rg_conv_w, rg_conv_b3, gate_w_b, rg_gate_b,
            rg_lambda3, sc_conv_w, l, batch, seq, tt)
        xp = _merge(xp, o, y_rg, y_sc, zr, norm_g, w_a_b, w_r_b, w_s_b, w_o_b, l, tm, gz_col)
        xp = _ffn(xp, norm_g, w_up_b, w_dn_b, l, 1, tm_wide)
        p_states.append((k.reshape(batch, seq, N_KV_HEADS, 2, head_dim),
                         v.reshape(batch, seq, N_KV_HEADS, hd2),
                         h_p.reshape(batch, w), rgb_p, scb_p))

        xs = _ffn(xs, norm_g, w_up_b, w_dn_b, l, 0, nb)
        q, k, v, zr = _inproj(xs, norm_g, w_in_b, l, nb, qw, kw, scale)
        q5 = q.reshape(nb, N_KV_HEADS, HEAD_GROUP, 2, head_dim)
        qt = jnp.einsum('bkgmd,kK,mM->bkgmKMd', q5, eye_kv, eye_m).reshape(nb, 2 * N_HEADS, kw)
        o = _attn_paged(qt, k.reshape(nb, 1, kw), v.reshape(nb, 1, kw), cache_kt, cache_v2,
                        page_table, slopes, lambda_qk, subln_g3, l, pps, lam_init)
        o = o.reshape(nb, qw)
        y_rg, y_sc, h_s, rgb_s, scb_s = _mixer_sample(
            zr, state_rglru_h[l], state_rglru_conv[l], state_sconv[l], rg_conv_w, rg_conv_b3,
            gate_w_b, rg_gate_b, rg_lambda3, sc_conv_w, l)
        xs = _merge(xs, o, y_rg, y_sc, zr, norm_g, w_a_b, w_r_b, w_s_b, w_o_b, l, nb, gz_col)
        xs = _ffn(xs, norm_g, w_up_b, w_dn_b, l, 1, nb)
        s_states.append((k.reshape(nb, 1, N_KV_HEADS, 2, head_dim),
                         v.reshape(nb, 1, N_KV_HEADS, hd2), h_s, rgb_s, scb_s))

    k_p, v_p, h_p, rgc_p, sc_p = [jnp.stack(s, axis=0) for s in zip(*p_states)]
    k_s, v_s, h_s, rgc_s, sc_s = [jnp.stack(s, axis=0) for s in zip(*s_states)]
    return (xp.reshape(batch, seq, d), xs.reshape(nb, 1, d), k_p, v_p, h_p, rgc_p, sc_p,
            k_s, v_s, h_s, rgc_s, sc_s)
```

```python
import functools
import math

import numpy as np
import jax
import jax.numpy as jnp
from jax import lax
from jax.experimental import pallas as pl
from jax.experimental.pallas import tpu as pltpu

F32 = jnp.float32
BF16 = jnp.bfloat16

NORM_EPS = 1e-6
LRU_C = 8.0
N_HEADS = 8
N_KV_HEADS = 4
HEAD_GROUP = N_HEADS // N_KV_HEADS
LRU_BLOCKS = 8
N_BRANCHES = 3
MASK_VALUE = -1e30
LOG2_E = math.log2(math.e)
VMEM_LIMIT_BYTES = 56 * 1024 * 1024


def _cparams(*sem):
    return pltpu.CompilerParams(dimension_semantics=sem, vmem_limit_bytes=VMEM_LIMIT_BYTES)


def _rms(x, g):
    return x * lax.rsqrt(jnp.mean(x * x, axis=-1, keepdims=True) + NORM_EPS) * g


def _dot(a, b):
    return jnp.dot(a, b, preferred_element_type=F32)


def _dot_nt(a, b):
    return lax.dot_general(a, b, (((1,), (1,)), ((), ())), preferred_element_type=F32)


def _sigmoid(x):
    return 0.5 * jnp.tanh(0.5 * x) + 0.5


def _gelu_tanh(x):
    c = math.sqrt(2.0 / math.pi)
    return 0.5 * x * (1.0 + jnp.tanh(c * (x + 0.044715 * (x * x * x))))


def _resident(shape, index_map):
    return pl.BlockSpec(shape, index_map, pipeline_mode=pl.Buffered(1))


def _ffn_kernel(x_ref, g_ref, wup_ref, wdn_ref, o_ref, *, d_ff, chunks, g_pre, g_post):
    x = x_ref[...]
    h = _rms(x, g_ref[g_pre:g_pre + 1, :]).astype(BF16)
    acc = None
    for c0, cw in chunks:
        gate = _dot(h, wup_ref[:, c0:c0 + cw])
        up = _dot(h, wup_ref[:, d_ff + c0:d_ff + c0 + cw])
        act = (gate * _sigmoid(gate) * up).astype(BF16)
        part = _dot(act, wdn_ref[c0:c0 + cw, :])
        acc = part if acc is None else acc + part
    o_ref[...] = x + 0.5 * _rms(acc, g_ref[g_post:g_post + 1, :])


def _ffn(x, norm_g, w_up, w_dn, layer, which, tm):
    rows, d = x.shape
    d_ff = w_dn.shape[2]
    chunk = 1024
    chunks = tuple((c0, min(chunk, d_ff - c0)) for c0 in range(0, d_ff, chunk))
    kern = functools.partial(_ffn_kernel, d_ff=d_ff, chunks=chunks,
                             g_pre=0 if which == 0 else 4, g_post=1 if which == 0 else 5)
    return pl.pallas_call(
        kern,
        grid=(rows // tm,),
        in_specs=[
            pl.BlockSpec((tm, d), lambda i: (i, 0)),
            _resident((None, 6, d), lambda i: (layer, 0, 0)),
            _resident((None, None, d, 2 * d_ff), lambda i: (layer, which, 0, 0)),
            _resident((None, None, d_ff, d), lambda i: (layer, which, 0, 0)),
        ],
        out_specs=pl.BlockSpec((tm, d), lambda i: (i, 0)),
        out_shape=jax.ShapeDtypeStruct((rows, d), F32),
        compiler_params=_cparams("parallel"),
        name=f"ffn{which}",
    )(x, norm_g, w_up, w_dn)


def _qkv_kernel(x_ref, g_ref, w_ref, q_ref, k_ref, v_ref, *, qw, kw, scale):
    h = _rms(x_ref[...], g_ref[2:3, :]).astype(BF16)
    z = _dot(h, w_ref[...])
    q_ref[...] = (z[:, :qw] * scale).astype(BF16)
    k_ref[...] = z[:, qw:qw + kw]
    v_ref[...] = z[:, qw + kw:]


def _zrest_kernel(x_ref, g_ref, w_ref, z_ref):
    h = _rms(x_ref[...], g_ref[2:3, :]).astype(BF16)
    z_ref[...] = _dot(h, w_ref[...])


def _inproj(x, norm_g, w_in, layer, tm, qw, kw, scale):
    rows, d = x.shape
    in_w = w_in.shape[2]
    cw = qw + 2 * kw
    n_rest = (in_w - cw) // cw
    assert cw * (n_rest + 1) == in_w
    q, k, v = pl.pallas_call(
        functools.partial(_qkv_kernel, qw=qw, kw=kw, scale=scale),
        grid=(rows // tm,),
        in_specs=[
            pl.BlockSpec((tm, d), lambda i: (i, 0)),
            _resident((None, 6, d), lambda i: (layer, 0, 0)),
            _resident((None, d, cw), lambda i: (layer, 0, 0)),
        ],
        out_specs=[
            pl.BlockSpec((tm, qw), lambda i: (i, 0)),
            pl.BlockSpec((tm, kw), lambda i: (i, 0)),
            pl.BlockSpec((tm, kw), lambda i: (i, 0)),
        ],
        out_shape=[
            jax.ShapeDtypeStruct((rows, qw), BF16),
            jax.ShapeDtypeStruct((rows, kw), F32),
            jax.ShapeDtypeStruct((rows, kw), F32),
        ],
        compiler_params=_cparams("parallel"),
        name="inproj_qkv",
    )(x, norm_g, w_in)
    tr = tm
    zr = pl.pallas_call(
        _zrest_kernel,
        grid=(n_rest, rows // tr),
        in_specs=[
            pl.BlockSpec((tr, d), lambda j, i: (i, 0)),
            _resident((None, 6, d), lambda j, i: (layer, 0, 0)),
            pl.BlockSpec((None, d, cw), lambda j, i: (layer, 0, j + 1)),
        ],
        out_specs=pl.BlockSpec((tr, cw), lambda j, i: (i, j)),
        out_shape=jax.ShapeDtypeStruct((rows, in_w - cw), F32),
        compiler_params=_cparams("parallel", "parallel"),
        name="inproj_rest",
    )(x, norm_g, w_in)
    return q, k, v, zr


def _lambda_full(lqk_ref, lam_init):
    s01 = jnp.sum(lqk_ref[0:1, :] * lqk_ref[1:2, :], axis=-1, keepdims=True)
    s23 = jnp.sum(lqk_ref[2:3, :] * lqk_ref[3:4, :], axis=-1, keepdims=True)
    return jnp.exp(s01) - jnp.exp(s23) + lam_init


def _subln(o, sg, lam_init):
    return _rms(o, sg) * (1.0 - lam_init)


def _loop_unrolled(n, fn, unroll):
    def body(jj, carry):
        for u in range(unroll):
            fn(unroll * jj + u)
        return carry

    lax.fori_loop(0, n // unroll, body, 0)
    if unroll == 4:
        base = (n // 4) * 4

        @pl.when(n % 4 >= 2)
        def _():
            fn(base)
            fn(base + 1)

    @pl.when(n % 2 == 1)
    def _():
        fn(n - 1)


def _attn_kernel(slopes_ref, q_ref, k_ref, v_ref, lqk_ref, sg_ref, o_ref,
                 kaug, vt, s_scr, mx_scr, ls_scr, acc_scr, *, tq, lam_init):
    kvh = pl.program_id(1)
    seq, hd2 = k_ref.shape
    hd = hd2 // 2
    tk = tq
    nq = seq // tq
    nqc = 2 * tq
    lane = lax.broadcasted_iota(jnp.int32, (1, hd2), 1)

    pos = lax.broadcasted_iota(jnp.int32, (seq, 1), 0)
    lo = pos & 7
    hi = (pos - lo).astype(F32)
    lo = lo.astype(F32)
    k = k_ref[...]
    kaug[0] = jnp.where(lane < hd, k, jnp.where(lane == hd, hi,
                        jnp.where(lane == hd + 1, lo, 0.0))).astype(BF16)
    kaug[1] = jnp.where(lane >= hd, k, jnp.where(lane == 0, hi,
                        jnp.where(lane == 1, lo, 0.0))).astype(BF16)
    for t in range(seq // tk):
        vt[t] = v_ref[t * tk:(t + 1) * tk, :].T.astype(BF16)

    row = lax.broadcasted_iota(jnp.int32, (2 * tq, 1), 0)
    slope = jnp.where(row < tq, slopes_ref[2 * kvh], slopes_ref[2 * kvh + 1])
    qcol = lax.broadcasted_iota(jnp.int32, (1, nqc), 1)
    diagonal_visible = (lax.broadcasted_iota(jnp.int32, (tk, 1), 0)
                        <= jnp.where(qcol < tq, qcol, qcol - tq))
    lam = _lambda_full(lqk_ref, lam_init)

    def q_tile(i):
        q = q_ref[pl.ds(pl.multiple_of(i * tq, tq), tq), :].astype(F32)
        qs = jnp.concatenate([q[:, :hd2], q[:, hd2:]], axis=0)
        return (jnp.where(lane < hd, qs, jnp.where(lane < hd + 2, slope, 0.0)).astype(BF16),
                jnp.where(lane >= hd, qs, jnp.where(lane < 2, slope, 0.0)).astype(BF16))

    def scores(qm, buf, j, on_diagonal):
        k0 = pl.multiple_of(j * tk, tk)
        for mm in range(2):
            s = _dot_nt(kaug[mm, pl.ds(k0, tk), :], qm[mm]) * LOG2_E
            if on_diagonal:
                s = jnp.where(diagonal_visible, s, MASK_VALUE)
            s_scr[buf, mm, j] = s
            mx_scr[buf, mm] = jnp.maximum(mx_scr[buf, mm],
                                          jnp.max(s.reshape(tk // 8, 8, nqc), axis=0))

    def weigh(buf, j):
        for mm in range(2):
            s = s_scr[buf, mm, j].reshape(tk // 8, 8, nqc)
            p = jnp.exp2(s - mx_scr[buf, mm])
            ls_scr[mm] = ls_scr[mm] + jnp.sum(p, axis=0)
            acc_scr[mm] = acc_scr[mm] + _dot(vt[j], p.reshape(tk, nqc).astype(BF16))

    def open_scores(buf):
        mx_scr[buf] = jnp.full(mx_scr.shape[1:], MASK_VALUE, F32)

    def close_scores(buf):
        for mm in range(2):
            mx_scr[buf, mm] = jnp.broadcast_to(
                jnp.max(mx_scr[buf, mm], axis=0, keepdims=True), mx_scr.shape[2:])
        ls_scr[...] = jnp.zeros(ls_scr.shape, F32)
        acc_scr[...] = jnp.zeros(acc_scr.shape, F32)

    def emit(i):
        l0 = jnp.sum(ls_scr[0], axis=0, keepdims=True)
        l1 = jnp.sum(ls_scr[1], axis=0, keepdims=True)
        ot = acc_scr[0] * (1.0 / l0) - lam * (acc_scr[1] * (1.0 / l1))
        ms = jnp.sum(jnp.sum((ot * ot).reshape(hd2 // 8, 8, nqc), axis=0), axis=0, keepdims=True)
        ot = ot * lax.rsqrt(ms * (1.0 / hd2) + NORM_EPS) * (sg_ref[...] * (1.0 - lam_init))
        o = ot.T.astype(o_ref.dtype)
        r0 = pl.multiple_of(i * tq, tq)
        o_ref[pl.ds(r0, tq), :hd2] = o[:tq]
        o_ref[pl.ds(r0, tq), hd2:] = o[tq:]

    open_scores(0)
    scores(q_tile(0), 0, 0, True)

    def q_step(i, carry):
        cur = i % 2
        nxt = 1 - cur
        close_scores(cur)
        q_next = q_tile(i + 1)
        open_scores(nxt)

        def both(j):
            weigh(cur, j)
            scores(q_next, nxt, j, False)

        _loop_unrolled(i + 1, both, 2)
        scores(q_next, nxt, i + 1, True)
        emit(i)
        return carry

    lax.fori_loop(0, nq - 1, q_step, 0)
    last = (nq - 1) % 2
    close_scores(last)
    _loop_unrolled(nq, lambda j: weigh(last, j), 4)
    emit(nq - 1)


def _attn_prompt(q, k, v, slopes, lambda_qk, subln_g, layer, batch, seq, tq, lam_init):
    rows, qw = q.shape
    hd2 = k.shape[1] // N_KV_HEADS
    nq = seq // tq
    assert tq % 128 == 0 and seq % tq == 0
    kern = functools.partial(_attn_kernel, tq=tq, lam_init=lam_init)
    return pl.pallas_call(
        kern,
        grid_spec=pltpu.PrefetchScalarGridSpec(
            num_scalar_prefetch=1,
            grid=(batch, N_KV_HEADS),
            in_specs=[
                pl.BlockSpec((seq, HEAD_GROUP * hd2), lambda b, h, s: (b, h)),
                pl.BlockSpec((seq, hd2), lambda b, h, s: (b, h)),
                pl.BlockSpec((seq, hd2), lambda b, h, s: (b, h)),
                pl.BlockSpec((None, 4, hd2 // 2), lambda b, h, s: (layer, 0, 0)),
                pl.BlockSpec((None, hd2, 1), lambda b, h, s: (layer, 0, 0)),
            ],
            out_specs=pl.BlockSpec((seq, HEAD_GROUP * hd2), lambda b, h, s: (b, h)),
            scratch_shapes=[
                pltpu.VMEM((2, seq, hd2), BF16),
                pltpu.VMEM((nq, hd2, tq), BF16),
                pltpu.VMEM((2, 2, nq, tq, 2 * tq), F32),
                pltpu.VMEM((2, 2, 8, 2 * tq), F32),
                pltpu.VMEM((2, 8, 2 * tq), F32),
                pltpu.VMEM((2, hd2, 2 * tq), F32),
            ],
        ),
        out_shape=jax.ShapeDtypeStruct((rows, qw), BF16),
        compiler_params=_cparams("parallel", "parallel"),
        name="attn_prompt",
    )(slopes, q, k, v, lambda_qk, subln_g)


def _paged_kernel(pt_ref, slopes_ref, qt_ref, ks_ref, vs_ref, lqk_ref, sg_ref, *refs,
                  pages_per_step, page, past, lam_init):
    del pt_ref
    pp = pages_per_step
    kt_refs = refs[:pp]
    v_refs = refs[pp:2 * pp]
    o_ref = refs[2 * pp]
    m_ref, l_ref, acc_ref = refs[2 * pp + 1:]
    c = pl.program_id(1)
    nrow = qt_ref.shape[0]
    hd2 = sg_ref.shape[1]
    qt = qt_ref[...]
    row = lax.broadcasted_iota(jnp.int32, (nrow, 1), 0)
    row_kv = row // (2 * HEAD_GROUP)
    head = row // 2
    slope = jnp.zeros((nrow, 1), F32)
    for h in range(N_HEADS):
        slope = jnp.where(head == h, slopes_ref[h], slope)

    @pl.when(c == 0)
    def _():
        ks = ks_ref[...].astype(BF16).astype(F32)
        m_ref[...] = jnp.sum(qt.astype(F32) * ks, axis=-1, keepdims=True)
        l_ref[...] = jnp.ones(l_ref.shape, F32)
        vs = vs_ref[...].astype(BF16).astype(F32)
        a = jnp.zeros(acc_ref.shape, F32)
        for kv in range(N_KV_HEADS):
            a = jnp.where(row_kv == kv, vs[:, kv * hd2:(kv + 1) * hd2], a)
        acc_ref[...] = a

    s_parts = []
    for r in range(pp):
        kpos = (c * pp + r) * page + lax.broadcasted_iota(jnp.int32, (1, page), 1)
        dist = (past - kpos).astype(F32)
        s_parts.append(_dot(qt, kt_refs[r][...].astype(BF16)) - slope * dist)
    s = jnp.concatenate(s_parts, axis=-1)
    m_old = m_ref[...]
    m_new = jnp.maximum(m_old, jnp.max(s, axis=-1, keepdims=True))
    p = jnp.exp(s - m_new)
    corr = jnp.exp(m_old - m_new)
    l_ref[...] = corr * l_ref[...] + jnp.sum(p, axis=-1, keepdims=True)
    pv = None
    for kv in range(N_KV_HEADS):
        pk = jnp.where(row_kv == kv, p, 0.0).astype(BF16)
        for r in range(pp):
            v_kv = v_refs[r][pl.ds(kv, page, stride=N_KV_HEADS), :].astype(BF16)
            part = _dot(pk[:, r * page:(r + 1) * page], v_kv)
            pv = part if pv is None else pv + part
    acc_ref[...] = corr * acc_ref[...] + pv
    m_ref[...] = m_new

    @pl.when(c == pl.num_programs(1) - 1)
    def _():
        lam = _lambda_full(lqk_ref, lam_init)
        acc_ref[...] = acc_ref[...] / l_ref[...]
        a0 = acc_ref[pl.ds(0, N_HEADS, stride=2), :]
        a1 = acc_ref[pl.ds(1, N_HEADS, stride=2), :]
        o_ref[...] = _subln(a0 - lam * a1, sg_ref[...], lam_init).astype(o_ref.dtype)


def _attn_paged(qt, k_new, v_new, cache_kt, cache_v2, page_table, slopes, lambda_qk, subln_g,
                layer, pages_per_step, lam_init):
    nb, nrow, kvw = qt.shape
    page = cache_kt.shape[3]
    n_pages = page_table.shape[1]
    pp = pages_per_step
    hd2 = kvw // N_KV_HEADS
    assert cache_kt.shape[2] == kvw and cache_v2.shape[2:] == (page * N_KV_HEADS, hd2)
    kern = functools.partial(_paged_kernel, pages_per_step=pp, page=page,
                             past=n_pages * page, lam_init=lam_init)

    def page_spec(arr, r):
        return pl.BlockSpec((None, None) + arr.shape[2:],
                            lambda b, c, pt, s: (layer, pt[b, c * pp + r], 0, 0))

    return pl.pallas_call(
        kern,
        grid_spec=pltpu.PrefetchScalarGridSpec(
            num_scalar_prefetch=2,
            grid=(nb, n_pages // pp),
            in_specs=[
                pl.BlockSpec((None, nrow, kvw), lambda b, c, pt, s: (b, 0, 0)),
                pl.BlockSpec((None, 1, kvw), lambda b, c, pt, s: (b, 0, 0)),
                pl.BlockSpec((None, 1, kvw), lambda b, c, pt, s: (b, 0, 0)),
                pl.BlockSpec((None, 4, hd2 // 2), lambda b, c, pt, s: (layer, 0, 0)),
                pl.BlockSpec((None, 1, hd2), lambda b, c, pt, s: (layer, 0, 0)),
            ] + [page_spec(cache_kt, r) for r in range(pp)]
              + [page_spec(cache_v2, r) for r in range(pp)],
            out_specs=pl.BlockSpec((None, N_HEADS, hd2), lambda b, c, pt, s: (b, 0, 0)),
            scratch_shapes=[
                pltpu.VMEM((nrow, 1), F32),
                pltpu.VMEM((nrow, 1), F32),
                pltpu.VMEM((nrow, hd2), F32),
            ],
        ),
        out_shape=jax.ShapeDtypeStruct((nb, N_HEADS, hd2), BF16),
        compiler_params=_cparams("parallel", "arbitrary"),
        name="attn_paged",
    )(page_table, slopes, qt, k_new, v_new, lambda_qk, subln_g,
      *([cache_kt] * pp), *([cache_v2] * pp))


def _lru_gates(xc, gw_ref, gb_ref, lam_ref):
    w = xc.shape[1]
    bw = w // LRU_BLOCKS
    g0, g1 = [], []
    for n in range(LRU_BLOCKS):
        xb = xc[:, n * bw:(n + 1) * bw].astype(BF16)
        g0.append(_dot(xb, gw_ref[0, n]))
        g1.append(_dot(xb, gw_ref[1, n]))
    r = _sigmoid(jnp.concatenate(g0, axis=-1) + gb_ref[0:1, :])
    i = _sigmoid(jnp.concatenate(g1, axis=-1) + gb_ref[1:2, :])
    neg_lam = -lam_ref[...]
    softplus = jnp.maximum(neg_lam, 0.0) + jnp.log1p(jnp.exp(-jnp.abs(neg_lam)))
    a = jnp.exp2(r * ((-LRU_C * LOG2_E) * softplus))
    d = 1.0 - a * a
    root = jnp.where(d > 0.0, d * lax.rsqrt(d), 0.0)
    u = root * (i * xc)
    return a, u


def _grouped(x):
    return x.reshape(x.shape[0] // 8, 8, x.shape[1])


def _shift_rows(x3, prev8, s, sub):
    rolled = pltpu.roll(jnp.concatenate([prev8[None], x3], axis=0), s, 1)
    return jnp.where(sub >= s, rolled[1:], rolled[:-1])


def _scan_groups(a3, u3, h_in):
    sub = lax.broadcasted_iota(jnp.int32, (1, 8, 1), 1)
    for s in (1, 2, 4):
        keep = sub >= s
        a_sh = jnp.where(keep, pltpu.roll(a3, s, 1), 1.0)
        u_sh = jnp.where(keep, pltpu.roll(u3, s, 1), 0.0)
        u3 = a3 * u_sh + u3
        a3 = a3 * a_sh
    out = []
    h_prev = h_in
    for g in range(a3.shape[0]):
        hg = u3[g] + a3[g] * h_prev
        out.append(hg)
        h_prev = hg[7:8, :]
    return jnp.concatenate(out, axis=0)


def _mixer_kernel(rgx_ref, rgg_ref, scb_ref, scc_ref, scx_ref, h0_ref, rgbuf_ref, scbuf_ref,
                  cw_ref, cb_ref, gw_ref, gb_ref, lam_ref, scw_ref,
                  yrg_ref, ysc_ref, hout_ref, rgbuf_out_ref, scbuf_out_ref,
                  prev_rg, prev_sc, hcarry, *, tt):
    t = pl.program_id(1)
    nrg = rgbuf_ref.shape[0]
    nsc = scbuf_ref.shape[0]
    w = rgx_ref.shape[1]
    sub = lax.broadcasted_iota(jnp.int32, (1, 8, 1), 1)

    @pl.when(t == 0)
    def _():
        prev_rg[...] = jnp.zeros(prev_rg.shape, F32)
        prev_sc[...] = jnp.zeros(prev_sc.shape, F32)
        prev_rg[8 - nrg:8, :] = rgbuf_ref[...]
        prev_sc[8 - nsc:8, :] = scbuf_ref[...]
        hcarry[...] = h0_ref[...]

    x = rgx_ref[...]
    x3 = _grouped(x)
    prev = prev_rg[...]
    xc3 = cb_ref[...] + cw_ref[nrg:nrg + 1, :] * x3
    for s in range(1, nrg + 1):
        xc3 = xc3 + cw_ref[nrg - s:nrg - s + 1, :] * _shift_rows(x3, prev, s, sub)
    prev_rg[...] = x[tt - 8:tt, :]
    a, u = _lru_gates(xc3.reshape(tt, w), gw_ref, gb_ref, lam_ref)
    h = _scan_groups(_grouped(a), _grouped(u), hcarry[...])
    hcarry[...] = h[tt - 1:tt, :]
    yrg_ref[...] = (h * _gelu_tanh(rgg_ref[...])).astype(yrg_ref.dtype)

    cx = scc_ref[...] * scx_ref[...]
    cx3 = _grouped(cx)
    prev = prev_sc[...]
    y3 = scw_ref[nsc:nsc + 1, :] * cx3
    for s in range(1, nsc + 1):
        y3 = y3 + scw_ref[nsc - s:nsc - s + 1, :] * _shift_rows(cx3, prev, s, sub)
    prev_sc[...] = cx[tt - 8:tt, :]
    ysc_ref[...] = (scb_ref[...] * y3.reshape(tt, w)).astype(ysc_ref.dtype)

    @pl.when(t == pl.num_programs(1) - 1)
    def _():
        hout_ref[...] = hcarry[...]
        rgbuf_out_ref[...] = prev_rg[8 - nrg:8, :]
        scbuf_out_ref[...] = prev_sc[8 - nsc:8, :]


def _mixer_prompt(zr, h0, rg_buf, sc_buf, rg_conv_w, rg_conv_b, gate_w, rg_gate_b, rg_lambda,
                  sc_conv_w, layer, batch, seq, tt):
    rows = zr.shape[0]
    w = h0.shape[-1]
    nt = seq // tt
    nrg, nsc = rg_buf.shape[1], sc_buf.shape[1]

    def zcol(cidx):
        return pl.BlockSpec((tt, w), lambda b, t: (b * nt + t, cidx))

    def per_batch(n):
        return pl.BlockSpec((None, n, w), lambda b, t: (b, 0, 0))

    def per_layer(n):
        return pl.BlockSpec((None, n, w), lambda b, t: (layer, 0, 0))

    bw = w // LRU_BLOCKS
    return pl.pallas_call(
        functools.partial(_mixer_kernel, tt=tt),
        grid=(batch, nt),
        in_specs=[zcol(0), zcol(1), zcol(2), zcol(3), zcol(4),
                  per_batch(1), per_batch(nrg), per_batch(nsc),
                  per_layer(nrg + 1), per_layer(1),
                  pl.BlockSpec((None, 2, LRU_BLOCKS, bw, bw), lambda b, t: (layer, 0, 0, 0, 0)),
                  per_layer(2), per_layer(1), per_layer(nsc + 1)],
        out_specs=[pl.BlockSpec((tt, w), lambda b, t: (b * nt + t, 0)),
                   pl.BlockSpec((tt, w), lambda b, t: (b * nt + t, 0)),
                   per_batch(1), per_batch(nrg), per_batch(nsc)],
        out_shape=[jax.ShapeDtypeStruct((rows, w), BF16),
                   jax.ShapeDtypeStruct((rows, w), BF16),
                   jax.ShapeDtypeStruct((batch, 1, w), F32),
                   jax.ShapeDtypeStruct((batch, nrg, w), F32),
                   jax.ShapeDtypeStruct((batch, nsc, w), F32)],
        scratch_shapes=[pltpu.VMEM((8, w), F32), pltpu.VMEM((8, w), F32),
                        pltpu.VMEM((1, w), F32)],
        compiler_params=_cparams("parallel", "arbitrary"),
        name="mixer_prompt",
    )(zr, zr, zr, zr, zr, h0, rg_buf, sc_buf, rg_conv_w, rg_conv_b, gate_w, rg_gate_b,
      rg_lambda, sc_conv_w)


def _mixer_step_kernel(rgx_ref, rgg_ref, scb_ref, scc_ref, scx_ref, h0_ref, rgbuf_ref, scbuf_ref,
                       cw_ref, cb_ref, gw_ref, gb_ref, lam_ref, scw_ref,
                       yrg_ref, ysc_ref, hout_ref, rgbuf_out_ref, scbuf_out_ref, *, nrg, nsc):
    w = h0_ref.shape[1]
    x = rgx_ref[...]
    xc = cb_ref[...] + cw_ref[nrg:nrg + 1, :] * x
    for j in range(nrg):
        xc = xc + cw_ref[j:j + 1, :] * rgbuf_ref[:, j * w:(j + 1) * w]
    a, u = _lru_gates(xc, gw_ref, gb_ref, lam_ref)
    h = a * h0_ref[...] + u
    hout_ref[...] = h
    yrg_ref[...] = (h * _gelu_tanh(rgg_ref[...])).astype(yrg_ref.dtype)
    for j in range(nrg - 1):
        rgbuf_out_ref[:, j * w:(j + 1) * w] = rgbuf_ref[:, (j + 1) * w:(j + 2) * w]
    rgbuf_out_ref[:, (nrg - 1) * w:] = x

    cx = scc_ref[...] * scx_ref[...]
    y = scw_ref[nsc:nsc + 1, :] * cx
    for j in range(nsc):
        y = y + scw_ref[j:j + 1, :] * scbuf_ref[:, j * w:(j + 1) * w]
    ysc_ref[...] = (scb_ref[...] * y).astype(ysc_ref.dtype)
    for j in range(nsc - 1):
        scbuf_out_ref[:, j * w:(j + 1) * w] = scbuf_ref[:, (j + 1) * w:(j + 2) * w]
    scbuf_out_ref[:, (nsc - 1) * w:] = cx


def _mixer_sample(zr, h0, rg_buf, sc_buf, rg_conv_w, rg_conv_b, gate_w, rg_gate_b, rg_lambda,
                  sc_conv_w, layer):
    nb, w = h0.shape
    nrg, nsc = rg_buf.shape[1], sc_buf.shape[1]
    bw = w // LRU_BLOCKS

    def zcol(cidx):
        return pl.BlockSpec((nb, w), lambda i: (0, cidx))

    def full(n):
        return pl.BlockSpec((nb, n * w), lambda i: (0, 0))

    def per_layer(n):
        return pl.BlockSpec((None, n, w), lambda i: (layer, 0, 0))

    outs = pl.pallas_call(
        functools.partial(_mixer_step_kernel, nrg=nrg, nsc=nsc),
        grid=(1,),
        in_specs=[zcol(0), zcol(1), zcol(2), zcol(3), zcol(4),
                  full(1), full(nrg), full(nsc),
                  per_layer(nrg + 1), per_layer(1),
                  pl.BlockSpec((None, 2, LRU_BLOCKS, bw, bw), lambda i: (layer, 0, 0, 0, 0)),
                  per_layer(2), per_layer(1), per_layer(nsc + 1)],
        out_specs=[full(1), full(1), full(1), full(nrg), full(nsc)],
        out_shape=[jax.ShapeDtypeStruct((nb, w), BF16),
                   jax.ShapeDtypeStruct((nb, w), BF16),
                   jax.ShapeDtypeStruct((nb, w), F32),
                   jax.ShapeDtypeStruct((nb, nrg * w), F32),
                   jax.ShapeDtypeStruct((nb, nsc * w), F32)],
        compiler_params=_cparams("arbitrary"),
        name="mixer_sample",
    )(zr, zr, zr, zr, zr, h0, rg_buf.reshape(nb, nrg * w), sc_buf.reshape(nb, nsc * w),
      rg_conv_w, rg_conv_b, gate_w, rg_gate_b, rg_lambda, sc_conv_w)
    y_rg, y_sc, h, rgb, scb = outs
    return y_rg, y_sc, h, rgb.reshape(nb, nrg, w), scb.reshape(nb, nsc, w)


def _merge_kernel(x_ref, o_ref, yrg_ref, ysc_ref, gz0_ref, gz1_ref, gz2_ref, g_ref,
                  wa_ref, wr_ref, ws_ref, wo_ref, out_ref):
    m = _sigmoid(gz0_ref[...]) * _dot(o_ref[...], wa_ref[...])
    m = m + _sigmoid(gz1_ref[...]) * _dot(yrg_ref[...], wr_ref[...])
    m = m + _sigmoid(gz2_ref[...]) * _dot(ysc_ref[...], ws_ref[...])
    y = _dot(m.astype(BF16), wo_ref[...])
    out_ref[...] = x_ref[...] + _rms(y, g_ref[3:4, :])


def _merge(x, o, y_rg, y_sc, zr, norm_g, w_a, w_r, w_s, w_o, layer, tm, gz_col):
    rows, d = x.shape

    def rowblk(cidx=0):
        return pl.BlockSpec((tm, d), lambda i: (i, cidx))

    def wspec(arr):
        return _resident((None,) + arr.shape[1:], lambda i: (layer, 0, 0))

    return pl.pallas_call(
        _merge_kernel,
        grid=(rows // tm,),
        in_specs=[rowblk(), rowblk(), rowblk(), rowblk(),
                  rowblk(gz_col), rowblk(gz_col + 1), rowblk(gz_col + 2),
                  _resident((None, 6, d), lambda i: (layer, 0, 0)),
                  wspec(w_a), wspec(w_r), wspec(w_s), wspec(w_o)],
        out_specs=rowblk(),
        out_shape=jax.ShapeDtypeStruct((rows, d), F32),
        compiler_params=_cparams("parallel"),
        name="merge",
    )(x, o, y_rg, y_sc, zr, zr, zr, norm_g, w_a, w_r, w_s, w_o)


def _pick_tile(n, pref):
    t = min(n, pref)
    while n % t:
        t //= 2
    return t


def kernel(x_prompt, x_sample, cache_k, cache_v, page_table, state_rglru_h, state_rglru_conv, state_sconv, norm_g, w_ffn_up, w_ffn_down, w_in, lambda_qk, subln_g, rg_conv_w, rg_conv_b, rg_gate_w, rg_gate_b, rg_lambda, sc_conv_w, w_branch_attn, w_branch_rg, w_branch_sc, w_out):
    batch, seq, d = x_prompt.shape
    nb, dec_seq, _ = x_sample.shape
    assert dec_seq == 1, "the sample group carries one new token per sequence"
    depth = w_in.shape[0]
    n_pool, page = cache_k.shape[1], cache_k.shape[2]
    head_dim = cache_k.shape[-1]
    hd2 = 2 * head_dim
    qw = N_HEADS * hd2
    kw = N_KV_HEADS * hd2
    w = state_rglru_h.shape[-1]
    assert qw == d and w == d and qw + 2 * kw == 2 * d
    scale = head_dim ** -0.5
    assert math.log2(scale) == round(math.log2(scale)), "q pre-scaling must be exact in bf16"
    n_pages = page_table.shape[1]
    nrg, nsc = state_rglru_conv.shape[2], state_sconv.shape[2]

    w_up_b = w_ffn_up.astype(BF16)
    w_dn_b = w_ffn_down.astype(BF16)
    w_in_b = w_in.astype(BF16)
    w_a_b = w_branch_attn.astype(BF16)
    w_r_b = w_branch_rg.astype(BF16)
    w_s_b = w_branch_sc.astype(BF16)
    w_o_b = w_out.astype(BF16)
    gate_w_b = rg_gate_w.astype(BF16)

    slopes_np = np.float32(2.0) ** (-8.0 * np.arange(1, N_HEADS + 1, dtype=np.float32) / N_HEADS)
    assert all(math.frexp(float(s))[0] == 0.5 for s in slopes_np), "slopes must be bf16-exact"
    slopes = jnp.asarray(slopes_np, F32)
    cache_kt = jnp.transpose(cache_k, (0, 1, 3, 4, 5, 2)).reshape(depth, n_pool, kw, page)
    cache_v2 = cache_v.reshape(depth, n_pool, page * N_KV_HEADS, hd2)
    rg_conv_b3 = rg_conv_b.reshape(depth, 1, w)
    rg_lambda3 = rg_lambda.reshape(depth, 1, w)
    subln_g3 = subln_g.reshape(depth, 1, hd2)
    subln_gc = subln_g.reshape(depth, hd2, 1)
    eye_kv = jnp.eye(N_KV_HEADS, dtype=BF16)
    eye_m = jnp.eye(2, dtype=BF16)

    tm = _pick_tile(batch * seq, 512)
    tm_wide = _pick_tile(batch * seq, 1024)
    tq = _pick_tile(seq, 256)
    tt = _pick_tile(seq, 512)
    pps = _pick_tile(n_pages, 16)
    gz_col = 5

    xp = x_prompt.reshape(batch * seq, d)
    xs = x_sample.reshape(nb, d)
    zeros_h = jnp.zeros((batch, 1, w), F32)
    zeros_rg = jnp.zeros((batch, nrg, w), F32)
    zeros_sc = jnp.zeros((batch, nsc, w), F32)

    p_states, s_states = [], []
    for l in range(depth):
        lam_init = 0.8 - 0.6 * math.exp(-0.3 * l)
        dense = dict(norm_g=norm_g, layer=l)

        xp = _ffn(xp, norm_g, w_up_b, w_dn_b, l, 0, tm_wide)
        q, k, v, zr = _inproj(xp, norm_g, w_in_b, l, tm_wide, qw, kw, scale)
        o = _attn_prompt(q, k, v, slopes, lambda_qk, subln_gc, l, batch, seq, tq, lam_init)
        y_rg, y_sc, h_p, rgb_p, scb_p = _mixer_prompt(
            zr, zeros_h, zeros_rg, zeros_sc, rg_conv_w, rg_conv_b3, gate_w_b, rg_gate_b,
            rg_lambda3, sc_conv_w, l, batch, seq, tt)
        xp = _merge(xp, o, y_rg, y_sc, zr, norm_g, w_a_b, w_r_b, w_s_b, w_o_b, l, tm, gz_col)
        xp = _ffn(xp, norm_g, w_up_b, w_dn_b, l, 1, tm_wide)
        p_states.append((k.reshape(batch, seq, N_KV_HEADS, 2, head_dim),
                         v.reshape(batch, seq, N_KV_HEADS, hd2),
                         h_p.reshape(batch, w), rgb_p, scb_p))

        xs = _ffn(xs, norm_g, w_up_b, w_dn_b, l, 0, nb)
        q, k, v, zr = _inproj(xs, norm_g, w_in_b, l, nb, qw, kw, scale)
        q5 = q.reshape(nb, N_KV_HEADS, HEAD_GROUP, 2, head_dim)
        qt = jnp.einsum('bkgmd,kK,mM->bkgmKMd', q5, eye_kv, eye_m).reshape(nb, 2 * N_HEADS, kw)
        o = _attn_paged(qt, k.reshape(nb, 1, kw), v.reshape(nb, 1, kw), cache_kt, cache_v2,
                        page_table, slopes, lambda_qk, subln_g3, l, pps, lam_init)
        o = o.reshape(nb, qw)
        y_rg, y_sc, h_s, rgb_s, scb_s = _mixer_sample(
            zr, state_rglru_h[l], state_rglru_conv[l], state_sconv[l], rg_conv_w, rg_conv_b3,
            gate_w_b, rg_gate_b, rg_lambda3, sc_conv_w, l)
        xs = _merge(xs, o, y_rg, y_sc, zr, norm_g, w_a_b, w_r_b, w_s_b, w_o_b, l, nb, gz_col)
        xs = _ffn(xs, norm_g, w_up_b, w_dn_b, l, 1, nb)
        s_states.append((k.reshape(nb, 1, N_KV_HEADS, 2, head_dim),
                         v.reshape(nb, 1, N_KV_HEADS, hd2), h_s, rgb_s, scb_s))

    k_p, v_p, h_p, rgc_p, sc_p = [jnp.stack(s, axis=0) for s in zip(*p_states)]
    k_s, v_s, h_s, rgc_s, sc_s = [jnp.stack(s, axis=0) for s in zip(*s_states)]
    return (xp.reshape(batch, seq, d), xs.reshape(nb, 1, d), k_p, v_p, h_p, rgc_p, sc_p,
            k_s, v_s, h_s, rgc_s, sc_s)
```

```python
import functools
import math

import numpy as np
import jax
import jax.numpy as jnp
from jax import lax
from jax.experimental import pallas as pl
from jax.experimental.pallas import tpu as pltpu

F32 = jnp.float32
BF16 = jnp.bfloat16

NORM_EPS = 1e-6
LRU_C = 8.0
N_HEADS = 8
N_KV_HEADS = 4
HEAD_GROUP = N_HEADS // N_KV_HEADS
LRU_BLOCKS = 8
N_BRANCHES = 3
MASK_VALUE = -1e30
LOG2_E = math.log2(math.e)
VMEM_LIMIT_BYTES = 56 * 1024 * 1024


def _cparams(*sem):
    return pltpu.CompilerParams(dimension_semantics=sem, vmem_limit_bytes=VMEM_LIMIT_BYTES)


def _rms(x, g):
    return x * lax.rsqrt(jnp.mean(x * x, axis=-1, keepdims=True) + NORM_EPS) * g


def _dot(a, b):
    return jnp.dot(a, b, preferred_element_type=F32)


def _dot_nt(a, b):
    return lax.dot_general(a, b, (((1,), (1,)), ((), ())), preferred_element_type=F32)


def _sigmoid(x):
    return 0.5 * jnp.tanh(0.5 * x) + 0.5


def _gelu_tanh(x):
    c = math.sqrt(2.0 / math.pi)
    return 0.5 * x * (1.0 + jnp.tanh(c * (x + 0.044715 * (x * x * x))))


def _resident(shape, index_map):
    return pl.BlockSpec(shape, index_map, pipeline_mode=pl.Buffered(1))


def _ffn_kernel(x_ref, g_ref, wup_ref, wdn_ref, o_ref, *, d_ff, chunks, g_pre, g_post):
    x = x_ref[...]
    h = _rms(x, g_ref[g_pre:g_pre + 1, :]).astype(BF16)
    acc = None
    for c0, cw in chunks:
        gate = _dot(h, wup_ref[:, c0:c0 + cw])
        up = _dot(h, wup_ref[:, d_ff + c0:d_ff + c0 + cw])
        act = (gate * _sigmoid(gate) * up).astype(BF16)
        part = _dot(act, wdn_ref[c0:c0 + cw, :])
        acc = part if acc is None else acc + part
    o_ref[...] = x + 0.5 * _rms(acc, g_ref[g_post:g_post + 1, :])


def _ffn(x, norm_g, w_up, w_dn, layer, which, tm):
    rows, d = x.shape
    d_ff = w_dn.shape[2]
    chunk = 1024
    chunks = tuple((c0, min(chunk, d_ff - c0)) for c0 in range(0, d_ff, chunk))
    kern = functools.partial(_ffn_kernel, d_ff=d_ff, chunks=chunks,
                             g_pre=0 if which == 0 else 4, g_post=1 if which == 0 else 5)
    return pl.pallas_call(
        kern,
        grid=(rows // tm,),
        in_specs=[
            pl.BlockSpec((tm, d), lambda i: (i, 0)),
            _resident((None, 6, d), lambda i: (layer, 0, 0)),
            _resident((None, None, d, 2 * d_ff), lambda i: (layer, which, 0, 0)),
            _resident((None, None, d_ff, d), lambda i: (layer, which, 0, 0)),
        ],
        out_specs=pl.BlockSpec((tm, d), lambda i: (i, 0)),
        out_shape=jax.ShapeDtypeStruct((rows, d), F32),
        compiler_params=_cparams("parallel"),
        name=f"ffn{which}",
    )(x, norm_g, w_up, w_dn)


def _qkv_kernel(x_ref, g_ref, w_ref, q_ref, k_ref, v_ref, *, qw, kw, scale):
    h = _rms(x_ref[...], g_ref[2:3, :]).astype(BF16)
    z = _dot(h, w_ref[...])
    q_ref[...] = (z[:, :qw] * scale).astype(BF16)
    k_ref[...] = z[:, qw:qw + kw]
    v_ref[...] = z[:, qw + kw:]


def _zrest_kernel(x_ref, g_ref, w_ref, z_ref):
    h = _rms(x_ref[...], g_ref[2:3, :]).astype(BF16)
    z_ref[...] = _dot(h, w_ref[...])


def _inproj(x, norm_g, w_in, layer, tm, qw, kw, scale):
    rows, d = x.shape
    in_w = w_in.shape[2]
    cw = qw + 2 * kw
    n_rest = (in_w - cw) // cw
    assert cw * (n_rest + 1) == in_w
    q, k, v = pl.pallas_call(
        functools.partial(_qkv_kernel, qw=qw, kw=kw, scale=scale),
        grid=(rows // tm,),
        in_specs=[
            pl.BlockSpec((tm, d), lambda i: (i, 0)),
            _resident((None, 6, d), lambda i: (layer, 0, 0)),
            _resident((None, d, cw), lambda i: (layer, 0, 0)),
        ],
        out_specs=[
            pl.BlockSpec((tm, qw), lambda i: (i, 0)),
            pl.BlockSpec((tm, kw), lambda i: (i, 0)),
            pl.BlockSpec((tm, kw), lambda i: (i, 0)),
        ],
        out_shape=[
            jax.ShapeDtypeStruct((rows, qw), BF16),
            jax.ShapeDtypeStruct((rows, kw), F32),
            jax.ShapeDtypeStruct((rows, kw), F32),
        ],
        compiler_params=_cparams("parallel"),
        name="inproj_qkv",
    )(x, norm_g, w_in)
    tr = tm
    zr = pl.pallas_call(
        _zrest_kernel,
        grid=(n_rest, rows // tr),
        in_specs=[
            pl.BlockSpec((tr, d), lambda j, i: (i, 0)),
            _resident((None, 6, d), lambda j, i: (layer, 0, 0)),
            pl.BlockSpec((None, d, cw), lambda j, i: (layer, 0, j + 1)),
        ],
        out_specs=pl.BlockSpec((tr, cw), lambda j, i: (i, j)),
        out_shape=jax.ShapeDtypeStruct((rows, in_w - cw), F32),
        compiler_params=_cparams("parallel", "parallel"),
        name="inproj_rest",
    )(x, norm_g, w_in)
    return q, k, v, zr


def _lambda_full(lqk_ref, lam_init):
    s01 = jnp.sum(lqk_ref[0:1, :] * lqk_ref[1:2, :], axis=-1, keepdims=True)
    s23 = jnp.sum(lqk_ref[2:3, :] * lqk_ref[3:4, :], axis=-1, keepdims=True)
    return jnp.exp(s01) - jnp.exp(s23) + lam_init


def _subln(o, sg, lam_init):
    return _rms(o, sg) * (1.0 - lam_init)


def _loop_unrolled(n, fn, unroll):
    def body(jj, carry):
        for u in range(unroll):
            fn(unroll * jj + u)
        return carry

    lax.fori_loop(0, n // unroll, body, 0)
    if unroll == 4:
        base = (n // 4) * 4

        @pl.when(n % 4 >= 2)
        def _():
            fn(base)
            fn(base + 1)

    @pl.when(n % 2 == 1)
    def _():
        fn(n - 1)


def _attn_kernel(slopes_ref, q_ref, k_ref, v_ref, pos_ref, lqk_ref, sg_ref, o_ref,
                 kaug, vbf, s_scr, mx_scr, ls_scr, acc_scr, *, tq, lam_init):
    kvh = pl.program_id(1)
    seq, hd2 = k_ref.shape
    hd = hd2 // 2
    tk = tq
    nc = tk // 128
    nq = seq // tq
    lane = lax.broadcasted_iota(jnp.int32, (1, hd2), 1)

    k = k_ref[...]
    kaug[0] = jnp.where(lane < hd, k, pos_ref[0]).astype(BF16)
    kaug[1] = jnp.where(lane >= hd, k, pos_ref[1]).astype(BF16)
    vbf[...] = v_ref[...].astype(BF16)

    row = lax.broadcasted_iota(jnp.int32, (2 * tq, 1), 0)
    slope = jnp.where(row < tq, slopes_ref[2 * kvh], slopes_ref[2 * kvh + 1])
    diagonal_visible = (lax.broadcasted_iota(jnp.int32, (1, tk), 1)
                        <= jnp.where(row < tq, row, row - tq))
    lam = _lambda_full(lqk_ref, lam_init)

    def q_tile(i):
        q = q_ref[pl.ds(pl.multiple_of(i * tq, tq), tq), :].astype(F32)
        qs = jnp.concatenate([q[:, :hd2], q[:, hd2:]], axis=0)
        return (jnp.where(lane < hd, qs, jnp.where(lane < hd + 2, slope, 0.0)).astype(BF16),
                jnp.where(lane >= hd, qs, jnp.where(lane < 2, slope, 0.0)).astype(BF16))

    def scores(qm, buf, j, on_diagonal):
        k0 = pl.multiple_of(j * tk, tk)
        for mm in range(2):
            s = _dot_nt(qm[mm], kaug[mm, pl.ds(k0, tk), :]) * LOG2_E
            if on_diagonal:
                s = jnp.where(diagonal_visible, s, MASK_VALUE)
            s_scr[buf, mm, j] = s
            mx = mx_scr[buf, mm]
            for c in range(nc):
                mx = jnp.maximum(mx, s[:, c * 128:(c + 1) * 128])
            mx_scr[buf, mm] = mx

    def weigh(buf, j):
        k0 = pl.multiple_of(j * tk, tk)
        vb = vbf[pl.ds(k0, tk), :]
        for mm in range(2):
            s = s_scr[buf, mm, j]
            mb = mx_scr[buf, mm]
            ps = [jnp.exp2(s[:, c * 128:(c + 1) * 128] - mb) for c in range(nc)]
            ls = ls_scr[mm]
            for c in range(nc):
                ls = ls + ps[c]
            ls_scr[mm] = ls
            p = jnp.concatenate(ps, axis=-1).astype(BF16)
            acc_scr[mm] = acc_scr[mm] + _dot(p, vb)

    def open_scores(buf):
        mx_scr[buf] = jnp.full(mx_scr.shape[1:], MASK_VALUE, F32)

    def close_scores(buf):
        for mm in range(2):
            mx_scr[buf, mm] = jnp.broadcast_to(
                jnp.max(mx_scr[buf, mm], axis=-1, keepdims=True), mx_scr.shape[2:])
        ls_scr[...] = jnp.zeros(ls_scr.shape, F32)
        acc_scr[...] = jnp.zeros(acc_scr.shape, F32)

    def emit(i):
        l0 = jnp.sum(ls_scr[0], axis=-1, keepdims=True)
        l1 = jnp.sum(ls_scr[1], axis=-1, keepdims=True)
        o = acc_scr[0] / l0 - lam * (acc_scr[1] / l1)
        o = _subln(o, sg_ref[...], lam_init).astype(o_ref.dtype)
        r0 = pl.multiple_of(i * tq, tq)
        o_ref[pl.ds(r0, tq), :hd2] = o[:tq]
        o_ref[pl.ds(r0, tq), hd2:] = o[tq:]

    open_scores(0)
    scores(q_tile(0), 0, 0, True)

    def q_step(i, carry):
        cur = i % 2
        nxt = 1 - cur
        close_scores(cur)
        q_next = q_tile(i + 1)
        open_scores(nxt)

        def both(j):
            weigh(cur, j)
            scores(q_next, nxt, j, False)

        _loop_unrolled(i + 1, both, 2)
        scores(q_next, nxt, i + 1, True)
        emit(i)
        return carry

    lax.fori_loop(0, nq - 1, q_step, 0)
    last = (nq - 1) % 2
    close_scores(last)
    _loop_unrolled(nq, lambda j: weigh(last, j), 4)
    emit(nq - 1)


def _attn_prompt(q, k, v, slopes, lambda_qk, subln_g, layer, batch, seq, tq, lam_init):
    rows, qw = q.shape
    hd2 = k.shape[1] // N_KV_HEADS
    nq = seq // tq
    assert tq % 128 == 0 and seq % tq == 0
    kern = functools.partial(_attn_kernel, tq=tq, lam_init=lam_init)
    hd = hd2 // 2
    pos = np.arange(seq)
    pos_lanes = np.zeros((2, seq, hd2), np.float32)
    pos_lanes[0, :, hd], pos_lanes[0, :, hd + 1] = pos - pos % 8, pos % 8
    pos_lanes[1, :, 0], pos_lanes[1, :, 1] = pos - pos % 8, pos % 8
    assert seq <= 8 * 256, "position parts must stay exact in bf16"
    return pl.pallas_call(
        kern,
        grid_spec=pltpu.PrefetchScalarGridSpec(
            num_scalar_prefetch=1,
            grid=(batch, N_KV_HEADS),
            in_specs=[
                pl.BlockSpec((seq, HEAD_GROUP * hd2), lambda b, h, s: (b, h)),
                pl.BlockSpec((seq, hd2), lambda b, h, s: (b, h)),
                pl.BlockSpec((seq, hd2), lambda b, h, s: (b, h)),
                _resident((2, seq, hd2), lambda b, h, s: (0, 0, 0)),
                pl.BlockSpec((None, 4, hd2 // 2), lambda b, h, s: (layer, 0, 0)),
                pl.BlockSpec((None, 1, hd2), lambda b, h, s: (layer, 0, 0)),
            ],
            out_specs=pl.BlockSpec((seq, HEAD_GROUP * hd2), lambda b, h, s: (b, h)),
            scratch_shapes=[
                pltpu.VMEM((2, seq, hd2), BF16),
                pltpu.VMEM((seq, hd2), BF16),
                pltpu.VMEM((2, 2, nq, 2 * tq, tq), F32),
                pltpu.VMEM((2, 2, 2 * tq, 128), F32),
                pltpu.VMEM((2, 2 * tq, 128), F32),
                pltpu.VMEM((2, 2 * tq, hd2), F32),
            ],
        ),
        out_shape=jax.ShapeDtypeStruct((rows, qw), BF16),
        compiler_params=_cparams("parallel", "parallel"),
        name="attn_prompt",
    )(slopes, q, k, v, jnp.asarray(pos_lanes), lambda_qk, subln_g)


def _paged_kernel(pt_ref, slopes_ref, qt_ref, ks_ref, vs_ref, lqk_ref, sg_ref, *refs,
                  pages_per_step, page, past, lam_init):
    del pt_ref
    pp = pages_per_step
    kt_refs = refs[:pp]
    v_refs = refs[pp:2 * pp]
    o_ref = refs[2 * pp]
    m_ref, l_ref, acc_ref = refs[2 * pp + 1:]
    c = pl.program_id(1)
    nrow = qt_ref.shape[0]
    hd2 = sg_ref.shape[1]
    qt = qt_ref[...]
    row = lax.broadcasted_iota(jnp.int32, (nrow, 1), 0)
    row_kv = row // (2 * HEAD_GROUP)
    head = row // 2
    slope = jnp.zeros((nrow, 1), F32)
    for h in range(N_HEADS):
        slope = jnp.where(head == h, slopes_ref[h], slope)

    @pl.when(c == 0)
    def _():
        ks = ks_ref[...].astype(BF16).astype(F32)
        m_ref[...] = jnp.sum(qt.astype(F32) * ks, axis=-1, keepdims=True)
        l_ref[...] = jnp.ones(l_ref.shape, F32)
        vs = vs_ref[...].astype(BF16).astype(F32)
        a = jnp.zeros(acc_ref.shape, F32)
        for kv in range(N_KV_HEADS):
            a = jnp.where(row_kv == kv, vs[:, kv * hd2:(kv + 1) * hd2], a)
        acc_ref[...] = a

    s_parts = []
    for r in range(pp):
        kpos = (c * pp + r) * page + lax.broadcasted_iota(jnp.int32, (1, page), 1)
        dist = (past - kpos).astype(F32)
        s_parts.append(_dot(qt, kt_refs[r][...].astype(BF16)) - slope * dist)
    s = jnp.concatenate(s_parts, axis=-1)
    m_old = m_ref[...]
    m_new = jnp.maximum(m_old, jnp.max(s, axis=-1, keepdims=True))
    p = jnp.exp(s - m_new)
    corr = jnp.exp(m_old - m_new)
    l_ref[...] = corr * l_ref[...] + jnp.sum(p, axis=-1, keepdims=True)
    pv = None
    for kv in range(N_KV_HEADS):
        pk = jnp.where(row_kv == kv, p, 0.0).astype(BF16)
        for r in range(pp):
            v_kv = v_refs[r][pl.ds(kv, page, stride=N_KV_HEADS), :].astype(BF16)
            part = _dot(pk[:, r * page:(r + 1) * page], v_kv)
            pv = part if pv is None else pv + part
    acc_ref[...] = corr * acc_ref[...] + pv
    m_ref[...] = m_new

    @pl.when(c == pl.num_programs(1) - 1)
    def _():
        lam = _lambda_full(lqk_ref, lam_init)
        acc_ref[...] = acc_ref[...] / l_ref[...]
        a0 = acc_ref[pl.ds(0, N_HEADS, stride=2), :]
        a1 = acc_ref[pl.ds(1, N_HEADS, stride=2), :]
        o_ref[...] = _subln(a0 - lam * a1, sg_ref[...], lam_init).astype(o_ref.dtype)


def _attn_paged(qt, k_new, v_new, cache_kt, cache_v2, page_table, slopes, lambda_qk, subln_g,
                layer, pages_per_step, lam_init):
    nb, nrow, kvw = qt.shape
    page = cache_kt.shape[3]
    n_pages = page_table.shape[1]
    pp = pages_per_step
    hd2 = kvw // N_KV_HEADS
    assert cache_kt.shape[2] == kvw and cache_v2.shape[2:] == (page * N_KV_HEADS, hd2)
    kern = functools.partial(_paged_kernel, pages_per_step=pp, page=page,
                             past=n_pages * page, lam_init=lam_init)

    def page_spec(arr, r):
        return pl.BlockSpec((None, None) + arr.shape[2:],
                            lambda b, c, pt, s: (layer, pt[b, c * pp + r], 0, 0))

    return pl.pallas_call(
        kern,
        grid_spec=pltpu.PrefetchScalarGridSpec(
            num_scalar_prefetch=2,
            grid=(nb, n_pages // pp),
            in_specs=[
                pl.BlockSpec((None, nrow, kvw), lambda b, c, pt, s: (b, 0, 0)),
                pl.BlockSpec((None, 1, kvw), lambda b, c, pt, s: (b, 0, 0)),
                pl.BlockSpec((None, 1, kvw), lambda b, c, pt, s: (b, 0, 0)),
                pl.BlockSpec((None, 4, hd2 // 2), lambda b, c, pt, s: (layer, 0, 0)),
                pl.BlockSpec((None, 1, hd2), lambda b, c, pt, s: (layer, 0, 0)),
            ] + [page_spec(cache_kt, r) for r in range(pp)]
              + [page_spec(cache_v2, r) for r in range(pp)],
            out_specs=pl.BlockSpec((None, N_HEADS, hd2), lambda b, c, pt, s: (b, 0, 0)),
            scratch_shapes=[
                pltpu.VMEM((nrow, 1), F32),
                pltpu.VMEM((nrow, 1), F32),
                pltpu.VMEM((nrow, hd2), F32),
            ],
        ),
        out_shape=jax.ShapeDtypeStruct((nb, N_HEADS, hd2), BF16),
        compiler_params=_cparams("parallel", "arbitrary"),
        name="attn_paged",
    )(page_table, slopes, qt, k_new, v_new, lambda_qk, subln_g,
      *([cache_kt] * pp), *([cache_v2] * pp))


def _lru_gates(xc, gw_ref, gb_ref, lam_ref):
    w = xc.shape[1]
    bw = w // LRU_BLOCKS
    g0, g1 = [], []
    for n in range(LRU_BLOCKS):
        xb = xc[:, n * bw:(n + 1) * bw].astype(BF16)
        g0.append(_dot(xb, gw_ref[0, n]))
        g1.append(_dot(xb, gw_ref[1, n]))
    t_r = jnp.tanh(jnp.concatenate(g0, axis=-1) + gb_ref[0:1, :])
    i = 0.5 * jnp.tanh(jnp.concatenate(g1, axis=-1) + gb_ref[1:2, :]) + 0.5
    neg_lam = -lam_ref[...]
    softplus = jnp.maximum(neg_lam, 0.0) + jnp.log1p(jnp.exp(-jnp.abs(neg_lam)))
    half_e = (-0.5 * LRU_C * LOG2_E) * softplus
    a = jnp.exp2(t_r * half_e + half_e)
    d = 1.0 - a * a
    root = jnp.where(d > 0.0, d * lax.rsqrt(d), 0.0)
    u = root * (i * xc)
    return a, u


def _grouped(x):
    return x.reshape(x.shape[0] // 8, 8, x.shape[1])


def _shift_rows(x3, prev8, s, sub):
    rolled = pltpu.roll(jnp.concatenate([prev8[None], x3], axis=0), s, 1)
    return jnp.where(sub >= s, rolled[1:], rolled[:-1])


def _scan_groups(a3, u3, h_in):
    sub = lax.broadcasted_iota(jnp.int32, (1, 8, 1), 1)
    for s in (1, 2, 4):
        keep = sub >= s
        a_sh = jnp.where(keep, pltpu.roll(a3, s, 1), 1.0)
        u_sh = jnp.where(keep, pltpu.roll(u3, s, 1), 0.0)
        u3 = a3 * u_sh + u3
        a3 = a3 * a_sh
    out = []
    h_prev = h_in
    for g in range(a3.shape[0]):
        hg = u3[g] + a3[g] * h_prev
        out.append(hg)
        h_prev = hg[7:8, :]
    return jnp.concatenate(out, axis=0)


def _mixer_kernel(rgx_ref, rgg_ref, scb_ref, scc_ref, scx_ref, h0_ref, rgbuf_ref, scbuf_ref,
                  cw_ref, cb_ref, gw_ref, gb_ref, lam_ref, scw_ref,
                  yrg_ref, ysc_ref, hout_ref, rgbuf_out_ref, scbuf_out_ref,
                  prev_rg, prev_sc, hcarry, *, tt):
    t = pl.program_id(1)
    nrg = rgbuf_ref.shape[0]
    nsc = scbuf_ref.shape[0]
    w = rgx_ref.shape[1]
    sub = lax.broadcasted_iota(jnp.int32, (1, 8, 1), 1)

    @pl.when(t == 0)
    def _():
        prev_rg[...] = jnp.zeros(prev_rg.shape, F32)
        prev_sc[...] = jnp.zeros(prev_sc.shape, F32)
        prev_rg[8 - nrg:8, :] = rgbuf_ref[...]
        prev_sc[8 - nsc:8, :] = scbuf_ref[...]
        hcarry[...] = h0_ref[...]

    x = rgx_ref[...]
    x3 = _grouped(x)
    prev = prev_rg[...]
    xc3 = cb_ref[...] + cw_ref[nrg:nrg + 1, :] * x3
    for s in range(1, nrg + 1):
        xc3 = xc3 + cw_ref[nrg - s:nrg - s + 1, :] * _shift_rows(x3, prev, s, sub)
    prev_rg[...] = x[tt - 8:tt, :]
    a, u = _lru_gates(xc3.reshape(tt, w), gw_ref, gb_ref, lam_ref)
    h = _scan_groups(_grouped(a), _grouped(u), hcarry[...])
    hcarry[...] = h[tt - 1:tt, :]
    yrg_ref[...] = (h * _gelu_tanh(rgg_ref[...])).astype(yrg_ref.dtype)

    cx = scc_ref[...] * scx_ref[...]
    cx3 = _grouped(cx)
    prev = prev_sc[...]
    y3 = scw_ref[nsc:nsc + 1, :] * cx3
    for s in range(1, nsc + 1):
        y3 = y3 + scw_ref[nsc - s:nsc - s + 1, :] * _shift_rows(cx3, prev, s, sub)
    prev_sc[...] = cx[tt - 8:tt, :]
    ysc_ref[...] = (scb_ref[...] * y3.reshape(tt, w)).astype(ysc_ref.dtype)

    @pl.when(t == pl.num_programs(1) - 1)
    def _():
        hout_ref[...] = hcarry[...]
        rgbuf_out_ref[...] = prev_rg[8 - nrg:8, :]
        scbuf_out_ref[...] = prev_sc[8 - nsc:8, :]


def _mixer_prompt(zr, h0, rg_buf, sc_buf, rg_conv_w, rg_conv_b, gate_w, rg_gate_b, rg_lambda,
                  sc_conv_w, layer, batch, seq, tt):
    rows = zr.shape[0]
    w = h0.shape[-1]
    nt = seq // tt
    nrg, nsc = rg_buf.shape[1], sc_buf.shape[1]

    def zcol(cidx):
        return pl.BlockSpec((tt, w), lambda b, t: (b * nt + t, cidx))

    def per_batch(n):
        return pl.BlockSpec((None, n, w), lambda b, t: (b, 0, 0))

    def per_layer(n):
        return pl.BlockSpec((None, n, w), lambda b, t: (layer, 0, 0))

    bw = w // LRU_BLOCKS
    return pl.pallas_call(
        functools.partial(_mixer_kernel, tt=tt),
        grid=(batch, nt),
        in_specs=[zcol(0), zcol(1), zcol(2), zcol(3), zcol(4),
                  per_batch(1), per_batch(nrg), per_batch(nsc),
                  per_layer(nrg + 1), per_layer(1),
                  pl.BlockSpec((None, 2, LRU_BLOCKS, bw, bw), lambda b, t: (layer, 0, 0, 0, 0)),
                  per_layer(2), per_layer(1), per_layer(nsc + 1)],
        out_specs=[pl.BlockSpec((tt, w), lambda b, t: (b * nt + t, 0)),
                   pl.BlockSpec((tt, w), lambda b, t: (b * nt + t, 0)),
                   per_batch(1), per_batch(nrg), per_batch(nsc)],
        out_shape=[jax.ShapeDtypeStruct((rows, w), BF16),
                   jax.ShapeDtypeStruct((rows, w), BF16),
                   jax.ShapeDtypeStruct((batch, 1, w), F32),
                   jax.ShapeDtypeStruct((batch, nrg, w), F32),
                   jax.ShapeDtypeStruct((batch, nsc, w), F32)],
        scratch_shapes=[pltpu.VMEM((8, w), F32), pltpu.VMEM((8, w), F32),
                        pltpu.VMEM((1, w), F32)],
        compiler_params=_cparams("parallel", "arbitrary"),
        name="mixer_prompt",
    )(zr, zr, zr, zr, zr, h0, rg_buf, sc_buf, rg_conv_w, rg_conv_b, gate_w, rg_gate_b,
      rg_lambda, sc_conv_w)


def _mixer_step_kernel(rgx_ref, rgg_ref, scb_ref, scc_ref, scx_ref, h0_ref, rgbuf_ref, scbuf_ref,
                       cw_ref, cb_ref, gw_ref, gb_ref, lam_ref, scw_ref,
                       yrg_ref, ysc_ref, hout_ref, rgbuf_out_ref, scbuf_out_ref, *, nrg, nsc):
    w = h0_ref.shape[1]
    x = rgx_ref[...]
    xc = cb_ref[...] + cw_ref[nrg:nrg + 1, :] * x
    for j in range(nrg):
        xc = xc + cw_ref[j:j + 1, :] * rgbuf_ref[:, j * w:(j + 1) * w]
    a, u = _lru_gates(xc, gw_ref, gb_ref, lam_ref)
    h = a * h0_ref[...] + u
    hout_ref[...] = h
    yrg_ref[...] = (h * _gelu_tanh(rgg_ref[...])).astype(yrg_ref.dtype)
    for j in range(nrg - 1):
        rgbuf_out_ref[:, j * w:(j + 1) * w] = rgbuf_ref[:, (j + 1) * w:(j + 2) * w]
    rgbuf_out_ref[:, (nrg - 1) * w:] = x

    cx = scc_ref[...] * scx_ref[...]
    y = scw_ref[nsc:nsc + 1, :] * cx
    for j in range(nsc):
        y = y + scw_ref[j:j + 1, :] * scbuf_ref[:, j * w:(j + 1) * w]
    ysc_ref[...] = (scb_ref[...] * y).astype(ysc_ref.dtype)
    for j in range(nsc - 1):
        scbuf_out_ref[:, j * w:(j + 1) * w] = scbuf_ref[:, (j + 1) * w:(j + 2) * w]
    scbuf_out_ref[:, (nsc - 1) * w:] = cx


def _mixer_sample(zr, h0, rg_buf, sc_buf, rg_conv_w, rg_conv_b, gate_w, rg_gate_b, rg_lambda,
                  sc_conv_w, layer):
    nb, w = h0.shape
    nrg, nsc = rg_buf.shape[1], sc_buf.shape[1]
    bw = w // LRU_BLOCKS

    def zcol(cidx):
        return pl.BlockSpec((nb, w), lambda i: (0, cidx))

    def full(n):
        return pl.BlockSpec((nb, n * w), lambda i: (0, 0))

    def per_layer(n):
        return pl.BlockSpec((None, n, w), lambda i: (layer, 0, 0))

    outs = pl.pallas_call(
        functools.partial(_mixer_step_kernel, nrg=nrg, nsc=nsc),
        grid=(1,),
        in_specs=[zcol(0), zcol(1), zcol(2), zcol(3), zcol(4),
                  full(1), full(nrg), full(nsc),
                  per_layer(nrg + 1), per_layer(1),
                  pl.BlockSpec((None, 2, LRU_BLOCKS, bw, bw), lambda i: (layer, 0, 0, 0, 0)),
                  per_layer(2), per_layer(1), per_layer(nsc + 1)],
        out_specs=[full(1), full(1), full(1), full(nrg), full(nsc)],
        out_shape=[jax.ShapeDtypeStruct((nb, w), BF16),
                   jax.ShapeDtypeStruct((nb, w), BF16),
                   jax.ShapeDtypeStruct((nb, w), F32),
                   jax.ShapeDtypeStruct((nb, nrg * w), F32),
                   jax.ShapeDtypeStruct((nb, nsc * w), F32)],
        compiler_params=_cparams("arbitrary"),
        name="mixer_sample",
    )(zr, zr, zr, zr, zr, h0, rg_buf.reshape(nb, nrg * w), sc_buf.reshape(nb, nsc * w),
      rg_conv_w, rg_conv_b, gate_w, rg_gate_b, rg_lambda, sc_conv_w)
    y_rg, y_sc, h, rgb, scb = outs
    return y_rg, y_sc, h, rgb.reshape(nb, nrg, w), scb.reshape(nb, nsc, w)


def _merge_kernel(x_ref, o_ref, yrg_ref, ysc_ref, gz0_ref, gz1_ref, gz2_ref, g_ref,
                  wa_ref, wr_ref, ws_ref, wo_ref, out_ref):
    m = _sigmoid(gz0_ref[...]) * _dot(o_ref[...], wa_ref[...])
    m = m + _sigmoid(gz1_ref[...]) * _dot(yrg_ref[...], wr_ref[...])
    m = m + _sigmoid(gz2_ref[...]) * _dot(ysc_ref[...], ws_ref[...])
    y = _dot(m.astype(BF16), wo_ref[...])
    out_ref[...] = x_ref[...] + _rms(y, g_ref[3:4, :])


def _merge(x, o, y_rg, y_sc, zr, norm_g, w_a, w_r, w_s, w_o, layer, tm, gz_col):
    rows, d = x.shape

    def rowblk(cidx=0):
        return pl.BlockSpec((tm, d), lambda i: (i, cidx))

    def wspec(arr):
        return _resident((None,) + arr.shape[1:], lambda i: (layer, 0, 0))

    return pl.pallas_call(
        _merge_kernel,
        grid=(rows // tm,),
        in_specs=[rowblk(), rowblk(), rowblk(), rowblk(),
                  rowblk(gz_col), rowblk(gz_col + 1), rowblk(gz_col + 2),
                  _resident((None, 6, d), lambda i: (layer, 0, 0)),
                  wspec(w_a), wspec(w_r), wspec(w_s), wspec(w_o)],
        out_specs=rowblk(),
        out_shape=jax.ShapeDtypeStruct((rows, d), F32),
        compiler_params=_cparams("parallel"),
        name="merge",
    )(x, o, y_rg, y_sc, zr, zr, zr, norm_g, w_a, w_r, w_s, w_o)


def _pick_tile(n, pref):
    t = min(n, pref)
    while n % t:
        t //= 2
    return t


def kernel(x_prompt, x_sample, cache_k, cache_v, page_table, state_rglru_h, state_rglru_conv, state_sconv, norm_g, w_ffn_up, w_ffn_down, w_in, lambda_qk, subln_g, rg_conv_w, rg_conv_b, rg_gate_w, rg_gate_b, rg_lambda, sc_conv_w, w_branch_attn, w_branch_rg, w_branch_sc, w_out):
    batch, seq, d = x_prompt.shape
    nb, dec_seq, _ = x_sample.shape
    assert dec_seq == 1, "the sample group carries one new token per sequence"
    depth = w_in.shape[0]
    n_pool, page = cache_k.shape[1], cache_k.shape[2]
    head_dim = cache_k.shape[-1]
    hd2 = 2 * head_dim
    qw = N_HEADS * hd2
    kw = N_KV_HEADS * hd2
    w = state_rglru_h.shape[-1]
    assert qw == d and w == d and qw + 2 * kw == 2 * d
    scale = head_dim ** -0.5
    assert math.log2(scale) == round(math.log2(scale)), "q pre-scaling must be exact in bf16"
    n_pages = page_table.shape[1]
    nrg, nsc = state_rglru_conv.shape[2], state_sconv.shape[2]

    w_up_b = w_ffn_up.astype(BF16)
    w_dn_b = w_ffn_down.astype(BF16)
    w_in_b = w_in.astype(BF16)
    w_a_b = w_branch_attn.astype(BF16)
    w_r_b = w_branch_rg.astype(BF16)
    w_s_b = w_branch_sc.astype(BF16)
    w_o_b = w_out.astype(BF16)
    gate_w_b = (0.5 * rg_gate_w).astype(BF16)
    gate_b_half = 0.5 * rg_gate_b

    slopes_np = np.float32(2.0) ** (-8.0 * np.arange(1, N_HEADS + 1, dtype=np.float32) / N_HEADS)
    assert all(math.frexp(float(s))[0] == 0.5 for s in slopes_np), "slopes must be bf16-exact"
    slopes = jnp.asarray(slopes_np, F32)
    cache_kt = jnp.transpose(cache_k, (0, 1, 3, 4, 5, 2)).reshape(depth, n_pool, kw, page)
    cache_v2 = cache_v.reshape(depth, n_pool, page * N_KV_HEADS, hd2)
    rg_conv_b3 = rg_conv_b.reshape(depth, 1, w)
    rg_lambda3 = rg_lambda.reshape(depth, 1, w)
    subln_g3 = subln_g.reshape(depth, 1, hd2)
    eye_kv = jnp.eye(N_KV_HEADS, dtype=BF16)
    eye_m = jnp.eye(2, dtype=BF16)

    tm = _pick_tile(batch * seq, 512)
    tm_wide = _pick_tile(batch * seq, 1024)
    tq = _pick_tile(seq, 256)
    tt = _pick_tile(seq, 512)
    pps = _pick_tile(n_pages, 32)
    gz_col = 5

    xp = x_prompt.reshape(batch * seq, d)
    xs = x_sample.reshape(nb, d)
    zeros_h = jnp.zeros((batch, 1, w), F32)
    zeros_rg = jnp.zeros((batch, nrg, w), F32)
    zeros_sc = jnp.zeros((batch, nsc, w), F32)

    p_states, s_states = [], []
    for l in range(depth):
        lam_init = 0.8 - 0.6 * math.exp(-0.3 * l)
        dense = dict(norm_g=norm_g, layer=l)

        xp = _ffn(xp, norm_g, w_up_b, w_dn_b, l, 0, tm_wide)
        q, k, v, zr = _inproj(xp, norm_g, w_in_b, l, tm_wide, qw, kw, scale)
        o = _attn_prompt(q, k, v, slopes, lambda_qk, subln_g3, l, batch, seq, tq, lam_init)
        y_rg, y_sc, h_p, rgb_p, scb_p = _mixer_prompt(
            zr, zeros_h, zeros_rg, zeros_sc, rg_conv_w, rg_conv_b3, gate_w_b, gate_b_half,
            rg_lambda3, sc_conv_w, l, batch, seq, tt)
        xp = _merge(xp, o, y_rg, y_sc, zr, norm_g, w_a_b, w_r_b, w_s_b, w_o_b, l, tm, gz_col)
        xp = _ffn(xp, norm_g, w_up_b, w_dn_b, l, 1, tm_wide)
        p_states.append((k.reshape(batch, seq, N_KV_HEADS, 2, head_dim),
                         v.reshape(batch, seq, N_KV_HEADS, hd2),
                         h_p.reshape(batch, w), rgb_p, scb_p))

        xs = _ffn(xs, norm_g, w_up_b, w_dn_b, l, 0, nb)
        q, k, v, zr = _inproj(xs, norm_g, w_in_b, l, nb, qw, kw, scale)
        q5 = q.reshape(nb, N_KV_HEADS, HEAD_GROUP, 2, head_dim)
        qt = jnp.einsum('bkgmd,kK,mM->bkgmKMd', q5, eye_kv, eye_m).reshape(nb, 2 * N_HEADS, kw)
        o = _attn_paged(qt, k.reshape(nb, 1, kw), v.reshape(nb, 1, kw), cache_kt, cache_v2,
                        page_table, slopes, lambda_qk, subln_g3, l, pps, lam_init)
        o = o.reshape(nb, qw)
        y_rg, y_sc, h_s, rgb_s, scb_s = _mixer_sample(
            zr, state_rglru_h[l], state_rglru_conv[l], state_sconv[l], rg_conv_w, rg_conv_b3,
            gate_w_b, gate_b_half, rg_lambda3, sc_conv_w, l)
        xs = _merge(xs, o, y_rg, y_sc, zr, norm_g, w_a_b, w_r_b, w_s_b, w_o_b, l, nb, gz_col)
        xs = _ffn(xs, norm_g, w_up_b, w_dn_b, l, 1, nb)
        s_states.append((k.reshape(nb, 1, N_KV_HEADS, 2, head_dim),
                         v.reshape(nb, 1, N_KV_HEADS, hd2), h_s, rgb_s, scb_s))

    k_p, v_p, h_p, rgc_p, sc_p = [jnp.stack(s, axis=0) for s in zip(*p_states)]
    k_s, v_s, h_s, rgc_s, sc_s = [jnp.stack(s, axis=0) for s in zip(*s_states)]
    return (xp.reshape(batch, seq, d), xs.reshape(nb, 1, d), k_p, v_p, h_p, rgc_p, sc_p,
            k_s, v_s, h_s, rgc_s, sc_s)
```

```python
import functools
import math

import numpy as np
import jax
import jax.numpy as jnp
from jax import lax
from jax.experimental import pallas as pl
from jax.experimental.pallas import tpu as pltpu

F32 = jnp.float32
BF16 = jnp.bfloat16

NORM_EPS = 1e-6
LRU_C = 8.0
N_HEADS = 8
N_KV_HEADS = 4
HEAD_GROUP = N_HEADS // N_KV_HEADS
LRU_BLOCKS = 8
N_BRANCHES = 3
MASK_VALUE = -1e30
LOG2_E = math.log2(math.e)
VMEM_LIMIT_BYTES = 56 * 1024 * 1024


def _cparams(*sem):
    return pltpu.CompilerParams(dimension_semantics=sem, vmem_limit_bytes=VMEM_LIMIT_BYTES)


def _rms(x, g):
    return x * lax.rsqrt(jnp.mean(x * x, axis=-1, keepdims=True) + NORM_EPS) * g


def _dot(a, b):
    return jnp.dot(a, b, preferred_element_type=F32)


def _dot_nt(a, b):
    return lax.dot_general(a, b, (((1,), (1,)), ((), ())), preferred_element_type=F32)


def _sigmoid(x):
    return 0.5 * jnp.tanh(0.5 * x) + 0.5


def _gelu_tanh(x):
    c = math.sqrt(2.0 / math.pi)
    return 0.5 * x * (1.0 + jnp.tanh(c * (x + 0.044715 * (x * x * x))))


def _resident(shape, index_map):
    return pl.BlockSpec(shape, index_map, pipeline_mode=pl.Buffered(1))


def _ffn_kernel(x_ref, g_ref, wup_ref, wdn_ref, o_ref, *, d_ff, chunks, g_pre, g_post):
    x = x_ref[...]
    h = _rms(x, g_ref[g_pre:g_pre + 1, :]).astype(BF16)
    acc = None
    for c0, cw in chunks:
        gate = _dot(h, wup_ref[:, c0:c0 + cw])
        up = _dot(h, wup_ref[:, d_ff + c0:d_ff + c0 + cw])
        act = (gate * _sigmoid(gate) * up).astype(BF16)
        part = _dot(act, wdn_ref[c0:c0 + cw, :])
        acc = part if acc is None else acc + part
    o_ref[...] = x + 0.5 * _rms(acc, g_ref[g_post:g_post + 1, :])


def _ffn(x, norm_g, w_up, w_dn, layer, which, tm):
    rows, d = x.shape
    d_ff = w_dn.shape[2]
    chunk = 1024
    chunks = tuple((c0, min(chunk, d_ff - c0)) for c0 in range(0, d_ff, chunk))
    kern = functools.partial(_ffn_kernel, d_ff=d_ff, chunks=chunks,
                             g_pre=0 if which == 0 else 4, g_post=1 if which == 0 else 5)
    return pl.pallas_call(
        kern,
        grid=(rows // tm,),
        in_specs=[
            pl.BlockSpec((tm, d), lambda i: (i, 0)),
            _resident((None, 6, d), lambda i: (layer, 0, 0)),
            _resident((None, None, d, 2 * d_ff), lambda i: (layer, which, 0, 0)),
            _resident((None, None, d_ff, d), lambda i: (layer, which, 0, 0)),
        ],
        out_specs=pl.BlockSpec((tm, d), lambda i: (i, 0)),
        out_shape=jax.ShapeDtypeStruct((rows, d), F32),
        compiler_params=_cparams("parallel"),
        name=f"ffn{which}",
    )(x, norm_g, w_up, w_dn)


def _qkv_kernel(x_ref, g_ref, w_ref, q_ref, k_ref, v_ref, *, qw, kw, scale):
    h = _rms(x_ref[...], g_ref[2:3, :]).astype(BF16)
    z = _dot(h, w_ref[...])
    q_ref[...] = (z[:, :qw] * scale).astype(BF16)
    k_ref[...] = z[:, qw:qw + kw]
    v_ref[...] = z[:, qw + kw:]


def _zrest_kernel(x_ref, g_ref, w_ref, z_ref):
    h = _rms(x_ref[...], g_ref[2:3, :]).astype(BF16)
    z_ref[...] = _dot(h, w_ref[...])


def _inproj(x, norm_g, w_in, layer, tm, qw, kw, scale):
    rows, d = x.shape
    in_w = w_in.shape[2]
    cw = qw + 2 * kw
    n_rest = (in_w - cw) // cw
    assert cw * (n_rest + 1) == in_w
    q, k, v = pl.pallas_call(
        functools.partial(_qkv_kernel, qw=qw, kw=kw, scale=scale),
        grid=(rows // tm,),
        in_specs=[
            pl.BlockSpec((tm, d), lambda i: (i, 0)),
            _resident((None, 6, d), lambda i: (layer, 0, 0)),
            _resident((None, d, cw), lambda i: (layer, 0, 0)),
        ],
        out_specs=[
            pl.BlockSpec((tm, qw), lambda i: (i, 0)),
            pl.BlockSpec((tm, kw), lambda i: (i, 0)),
            pl.BlockSpec((tm, kw), lambda i: (i, 0)),
        ],
        out_shape=[
            jax.ShapeDtypeStruct((rows, qw), BF16),
            jax.ShapeDtypeStruct((rows, kw), F32),
            jax.ShapeDtypeStruct((rows, kw), F32),
        ],
        compiler_params=_cparams("parallel"),
        name="inproj_qkv",
    )(x, norm_g, w_in)
    tr = tm
    zr = pl.pallas_call(
        _zrest_kernel,
        grid=(n_rest, rows // tr),
        in_specs=[
            pl.BlockSpec((tr, d), lambda j, i: (i, 0)),
            _resident((None, 6, d), lambda j, i: (layer, 0, 0)),
            pl.BlockSpec((None, d, cw), lambda j, i: (layer, 0, j + 1)),
        ],
        out_specs=pl.BlockSpec((tr, cw), lambda j, i: (i, j)),
        out_shape=jax.ShapeDtypeStruct((rows, in_w - cw), F32),
        compiler_params=_cparams("parallel", "parallel"),
        name="inproj_rest",
    )(x, norm_g, w_in)
    return q, k, v, zr


def _lambda_full(lqk_ref, lam_init):
    s01 = jnp.sum(lqk_ref[0:1, :] * lqk_ref[1:2, :], axis=-1, keepdims=True)
    s23 = jnp.sum(lqk_ref[2:3, :] * lqk_ref[3:4, :], axis=-1, keepdims=True)
    return jnp.exp(s01) - jnp.exp(s23) + lam_init


def _subln(o, sg, lam_init):
    return _rms(o, sg) * (1.0 - lam_init)


def _loop_unrolled(n, fn, unroll):
    def body(jj, carry):
        for u in range(unroll):
            fn(unroll * jj + u)
        return carry

    lax.fori_loop(0, n // unroll, body, 0)
    if unroll == 4:
        base = (n // 4) * 4

        @pl.when(n % 4 >= 2)
        def _():
            fn(base)
            fn(base + 1)

    @pl.when(n % 2 == 1)
    def _():
        fn(n - 1)


def _attn_kernel(slopes_ref, q_ref, k_ref, v_ref, pos_ref, lqk_ref, sg_ref, o_ref,
                 kaug, vbf, s_scr, mx_scr, ls_scr, acc_scr, *, tq, lam_init):
    kvh = pl.program_id(1)
    seq, hd2 = k_ref.shape
    hd = hd2 // 2
    tk = tq
    nc = tk // 128
    nq = seq // tq
    lane = lax.broadcasted_iota(jnp.int32, (1, hd2), 1)

    k = k_ref[...]
    kaug[0] = jnp.where(lane < hd, k, pos_ref[0]).astype(BF16)
    kaug[1] = jnp.where(lane >= hd, k, pos_ref[1]).astype(BF16)
    vbf[...] = v_ref[...].astype(BF16)

    row = lax.broadcasted_iota(jnp.int32, (2 * tq, 1), 0)
    slope = jnp.where(row < tq, slopes_ref[2 * kvh], slopes_ref[2 * kvh + 1])
    diagonal_visible = (lax.broadcasted_iota(jnp.int32, (1, tk), 1)
                        <= jnp.where(row < tq, row, row - tq))
    lam = _lambda_full(lqk_ref, lam_init)

    def q_tile(i):
        q = q_ref[pl.ds(pl.multiple_of(i * tq, tq), tq), :].astype(F32)
        qs = jnp.concatenate([q[:, :hd2], q[:, hd2:]], axis=0)
        return (jnp.where(lane < hd, qs, jnp.where(lane < hd + 2, slope, 0.0)).astype(BF16),
                jnp.where(lane >= hd, qs, jnp.where(lane < 2, slope, 0.0)).astype(BF16))

    def scores(qm, buf, j, on_diagonal):
        k0 = pl.multiple_of(j * tk, tk)
        for mm in range(2):
            s = _dot_nt(qm[mm], kaug[mm, pl.ds(k0, tk), :]) * LOG2_E
            if on_diagonal:
                s = jnp.where(diagonal_visible, s, MASK_VALUE)
            s_scr[buf, mm, j] = s
            mx = mx_scr[buf, mm]
            for c in range(nc):
                mx = jnp.maximum(mx, s[:, c * 128:(c + 1) * 128])
            mx_scr[buf, mm] = mx

    def weigh(buf, j):
        k0 = pl.multiple_of(j * tk, tk)
        vb = vbf[pl.ds(k0, tk), :]
        for mm in range(2):
            s = s_scr[buf, mm, j]
            mb = mx_scr[buf, mm]
            ps = [jnp.exp2(s[:, c * 128:(c + 1) * 128] - mb) for c in range(nc)]
            ls = ls_scr[mm]
            for c in range(nc):
                ls = ls + ps[c]
            ls_scr[mm] = ls
            p = jnp.concatenate(ps, axis=-1).astype(BF16)
            acc_scr[mm] = acc_scr[mm] + _dot(p, vb)

    def open_scores(buf):
        mx_scr[buf] = jnp.full(mx_scr.shape[1:], MASK_VALUE, F32)

    def close_scores(buf):
        for mm in range(2):
            mx_scr[buf, mm] = jnp.broadcast_to(
                jnp.max(mx_scr[buf, mm], axis=-1, keepdims=True), mx_scr.shape[2:])
        ls_scr[...] = jnp.zeros(ls_scr.shape, F32)
        acc_scr[...] = jnp.zeros(acc_scr.shape, F32)

    def emit(i):
        l0 = jnp.sum(ls_scr[0], axis=-1, keepdims=True)
        l1 = jnp.sum(ls_scr[1], axis=-1, keepdims=True)
        o = acc_scr[0] / l0 - lam * (acc_scr[1] / l1)
        o = _subln(o, sg_ref[...], lam_init).astype(o_ref.dtype)
        r0 = pl.multiple_of(i * tq, tq)
        o_ref[pl.ds(r0, tq), :hd2] = o[:tq]
        o_ref[pl.ds(r0, tq), hd2:] = o[tq:]

    open_scores(0)
    scores(q_tile(0), 0, 0, True)

    def q_step(i, carry):
        cur = i % 2
        nxt = 1 - cur
        close_scores(cur)
        q_next = q_tile(i + 1)
        open_scores(nxt)

        _loop_unrolled(i + 1, lambda j: weigh(cur, j), 4)
        _loop_unrolled(i + 1, lambda j: scores(q_next, nxt, j, False), 4)
        scores(q_next, nxt, i + 1, True)
        emit(i)
        return carry

    lax.fori_loop(0, nq - 1, q_step, 0)
    last = (nq - 1) % 2
    close_scores(last)
    _loop_unrolled(nq, lambda j: weigh(last, j), 4)
    emit(nq - 1)


def _attn_prompt(q, k, v, slopes, lambda_qk, subln_g, layer, batch, seq, tq, lam_init):
    rows, qw = q.shape
    hd2 = k.shape[1] // N_KV_HEADS
    nq = seq // tq
    assert tq % 128 == 0 and seq % tq == 0
    kern = functools.partial(_attn_kernel, tq=tq, lam_init=lam_init)
    hd = hd2 // 2
    pos = np.arange(seq)
    pos_lanes = np.zeros((2, seq, hd2), np.float32)
    pos_lanes[0, :, hd], pos_lanes[0, :, hd + 1] = pos - pos % 8, pos % 8
    pos_lanes[1, :, 0], pos_lanes[1, :, 1] = pos - pos % 8, pos % 8
    assert seq <= 8 * 256, "position parts must stay exact in bf16"
    return pl.pallas_call(
        kern,
        grid_spec=pltpu.PrefetchScalarGridSpec(
            num_scalar_prefetch=1,
            grid=(batch, N_KV_HEADS),
            in_specs=[
                pl.BlockSpec((seq, HEAD_GROUP * hd2), lambda b, h, s: (b, h)),
                pl.BlockSpec((seq, hd2), lambda b, h, s: (b, h)),
                pl.BlockSpec((seq, hd2), lambda b, h, s: (b, h)),
                _resident((2, seq, hd2), lambda b, h, s: (0, 0, 0)),
                pl.BlockSpec((None, 4, hd2 // 2), lambda b, h, s: (layer, 0, 0)),
                pl.BlockSpec((None, 1, hd2), lambda b, h, s: (layer, 0, 0)),
            ],
            out_specs=pl.BlockSpec((seq, HEAD_GROUP * hd2), lambda b, h, s: (b, h)),
            scratch_shapes=[
                pltpu.VMEM((2, seq, hd2), BF16),
                pltpu.VMEM((seq, hd2), BF16),
                pltpu.VMEM((2, 2, nq, 2 * tq, tq), F32),
                pltpu.VMEM((2, 2, 2 * tq, 128), F32),
                pltpu.VMEM((2, 2 * tq, 128), F32),
                pltpu.VMEM((2, 2 * tq, hd2), F32),
            ],
        ),
        out_shape=jax.ShapeDtypeStruct((rows, qw), BF16),
        compiler_params=_cparams("parallel", "parallel"),
        name="attn_prompt",
    )(slopes, q, k, v, jnp.asarray(pos_lanes), lambda_qk, subln_g)


def _paged_kernel(pt_ref, slopes_ref, qt_ref, ks_ref, vs_ref, lqk_ref, sg_ref, *refs,
                  pages_per_step, page, past, lam_init):
    del pt_ref
    pp = pages_per_step
    kt_refs = refs[:pp]
    v_refs = refs[pp:2 * pp]
    o_ref = refs[2 * pp]
    m_ref, l_ref, acc_ref = refs[2 * pp + 1:]
    c = pl.program_id(1)
    nrow = qt_ref.shape[0]
    hd2 = sg_ref.shape[1]
    qt = qt_ref[...]
    row = lax.broadcasted_iota(jnp.int32, (nrow, 1), 0)
    row_kv = row // (2 * HEAD_GROUP)
    head = row // 2
    slope = jnp.zeros((nrow, 1), F32)
    for h in range(N_HEADS):
        slope = jnp.where(head == h, slopes_ref[h], slope)

    @pl.when(c == 0)
    def _():
        ks = ks_ref[...].astype(BF16).astype(F32)
        m_ref[...] = jnp.sum(qt.astype(F32) * ks, axis=-1, keepdims=True)
        l_ref[...] = jnp.ones(l_ref.shape, F32)
        vs = vs_ref[...].astype(BF16).astype(F32)
        a = jnp.zeros(acc_ref.shape, F32)
        for kv in range(N_KV_HEADS):
            a = jnp.where(row_kv == kv, vs[:, kv * hd2:(kv + 1) * hd2], a)
        acc_ref[...] = a

    s_parts = []
    for r in range(pp):
        kpos = (c * pp + r) * page + lax.broadcasted_iota(jnp.int32, (1, page), 1)
        dist = (past - kpos).astype(F32)
        s_parts.append(_dot(qt, kt_refs[r][...].astype(BF16)) - slope * dist)
    s = jnp.concatenate(s_parts, axis=-1)
    m_old = m_ref[...]
    m_new = jnp.maximum(m_old, jnp.max(s, axis=-1, keepdims=True))
    p = jnp.exp(s - m_new)
    corr = jnp.exp(m_old - m_new)
    l_ref[...] = corr * l_ref[...] + jnp.sum(p, axis=-1, keepdims=True)
    pv = None
    for kv in range(N_KV_HEADS):
        pk = jnp.where(row_kv == kv, p, 0.0).astype(BF16)
        for r in range(pp):
            v_kv = v_refs[r][pl.ds(kv, page, stride=N_KV_HEADS), :].astype(BF16)
            part = _dot(pk[:, r * page:(r + 1) * page], v_kv)
            pv = part if pv is None else pv + part
    acc_ref[...] = corr * acc_ref[...] + pv
    m_ref[...] = m_new

    @pl.when(c == pl.num_programs(1) - 1)
    def _():
        lam = _lambda_full(lqk_ref, lam_init)
        acc_ref[...] = acc_ref[...] / l_ref[...]
        a0 = acc_ref[pl.ds(0, N_HEADS, stride=2), :]
        a1 = acc_ref[pl.ds(1, N_HEADS, stride=2), :]
        o_ref[...] = _subln(a0 - lam * a1, sg_ref[...], lam_init).astype(o_ref.dtype)


def _attn_paged(qt, k_new, v_new, cache_kt, cache_v2, page_table, slopes, lambda_qk, subln_g,
                layer, pages_per_step, lam_init):
    nb, nrow, kvw = qt.shape
    page = cache_kt.shape[3]
    n_pages = page_table.shape[1]
    pp = pages_per_step
    hd2 = kvw // N_KV_HEADS
    assert cache_kt.shape[2] == kvw and cache_v2.shape[2:] == (page * N_KV_HEADS, hd2)
    kern = functools.partial(_paged_kernel, pages_per_step=pp, page=page,
                             past=n_pages * page, lam_init=lam_init)

    def page_spec(arr, r):
        return pl.BlockSpec((None, None) + arr.shape[2:],
                            lambda b, c, pt, s: (layer, pt[b, c * pp + r], 0, 0))

    return pl.pallas_call(
        kern,
        grid_spec=pltpu.PrefetchScalarGridSpec(
            num_scalar_prefetch=2,
            grid=(nb, n_pages // pp),
            in_specs=[
                pl.BlockSpec((None, nrow, kvw), lambda b, c, pt, s: (b, 0, 0)),
                pl.BlockSpec((None, 1, kvw), lambda b, c, pt, s: (b, 0, 0)),
                pl.BlockSpec((None, 1, kvw), lambda b, c, pt, s: (b, 0, 0)),
                pl.BlockSpec((None, 4, hd2 // 2), lambda b, c, pt, s: (layer, 0, 0)),
                pl.BlockSpec((None, 1, hd2), lambda b, c, pt, s: (layer, 0, 0)),
            ] + [page_spec(cache_kt, r) for r in range(pp)]
              + [page_spec(cache_v2, r) for r in range(pp)],
            out_specs=pl.BlockSpec((None, N_HEADS, hd2), lambda b, c, pt, s: (b, 0, 0)),
            scratch_shapes=[
                pltpu.VMEM((nrow, 1), F32),
                pltpu.VMEM((nrow, 1), F32),
                pltpu.VMEM((nrow, hd2), F32),
            ],
        ),
        out_shape=jax.ShapeDtypeStruct((nb, N_HEADS, hd2), BF16),
        compiler_params=_cparams("parallel", "arbitrary"),
        name="attn_paged",
    )(page_table, slopes, qt, k_new, v_new, lambda_qk, subln_g,
      *([cache_kt] * pp), *([cache_v2] * pp))


def _lru_gates(xc, gw_ref, gb_ref, lam_ref):
    w = xc.shape[1]
    bw = w // LRU_BLOCKS
    g0, g1 = [], []
    for n in range(LRU_BLOCKS):
        xb = xc[:, n * bw:(n + 1) * bw].astype(BF16)
        g0.append(_dot(xb, gw_ref[0, n]))
        g1.append(_dot(xb, gw_ref[1, n]))
    t_r = jnp.tanh(jnp.concatenate(g0, axis=-1) + gb_ref[0:1, :])
    i = 0.5 * jnp.tanh(jnp.concatenate(g1, axis=-1) + gb_ref[1:2, :]) + 0.5
    neg_lam = -lam_ref[...]
    softplus = jnp.maximum(neg_lam, 0.0) + jnp.log1p(jnp.exp(-jnp.abs(neg_lam)))
    half_e = (-0.5 * LRU_C * LOG2_E) * softplus
    a = jnp.exp2(t_r * half_e + half_e)
    d = 1.0 - a * a
    root = jnp.where(d > 0.0, d * lax.rsqrt(d), 0.0)
    u = root * (i * xc)
    return a, u


def _grouped(x):
    return x.reshape(x.shape[0] // 8, 8, x.shape[1])


def _shift_rows(x3, prev8, s, sub):
    rolled = pltpu.roll(jnp.concatenate([prev8[None], x3], axis=0), s, 1)
    return jnp.where(sub >= s, rolled[1:], rolled[:-1])


def _scan_groups(a3, u3, h_in):
    sub = lax.broadcasted_iota(jnp.int32, (1, 8, 1), 1)
    for s in (1, 2, 4):
        keep = sub >= s
        a_sh = jnp.where(keep, pltpu.roll(a3, s, 1), 1.0)
        u_sh = jnp.where(keep, pltpu.roll(u3, s, 1), 0.0)
        u3 = a3 * u_sh + u3
        a3 = a3 * a_sh
    out = []
    h_prev = h_in
    for g in range(a3.shape[0]):
        hg = u3[g] + a3[g] * h_prev
        out.append(hg)
        h_prev = hg[7:8, :]
    return jnp.concatenate(out, axis=0)


def _mixer_kernel(rgx_ref, rgg_ref, scb_ref, scc_ref, scx_ref, h0_ref, rgbuf_ref, scbuf_ref,
                  cw_ref, cb_ref, gw_ref, gb_ref, lam_ref, scw_ref,
                  yrg_ref, ysc_ref, hout_ref, rgbuf_out_ref, scbuf_out_ref,
                  prev_rg, prev_sc, hcarry, *, tt):
    t = pl.program_id(1)
    nrg = rgbuf_ref.shape[0]
    nsc = scbuf_ref.shape[0]
    w = rgx_ref.shape[1]
    sub = lax.broadcasted_iota(jnp.int32, (1, 8, 1), 1)

    @pl.when(t == 0)
    def _():
        prev_rg[...] = jnp.zeros(prev_rg.shape, F32)
        prev_sc[...] = jnp.zeros(prev_sc.shape, F32)
        prev_rg[8 - nrg:8, :] = rgbuf_ref[...]
        prev_sc[8 - nsc:8, :] = scbuf_ref[...]
        hcarry[...] = h0_ref[...]

    x = rgx_ref[...]
    x3 = _grouped(x)
    prev = prev_rg[...]
    xc3 = cb_ref[...] + cw_ref[nrg:nrg + 1, :] * x3
    for s in range(1, nrg + 1):
        xc3 = xc3 + cw_ref[nrg - s:nrg - s + 1, :] * _shift_rows(x3, prev, s, sub)
    prev_rg[...] = x[tt - 8:tt, :]
    a, u = _lru_gates(xc3.reshape(tt, w), gw_ref, gb_ref, lam_ref)
    h = _scan_groups(_grouped(a), _grouped(u), hcarry[...])
    hcarry[...] = h[tt - 1:tt, :]
    yrg_ref[...] = (h * _gelu_tanh(rgg_ref[...])).astype(yrg_ref.dtype)

    cx = scc_ref[...] * scx_ref[...]
    cx3 = _grouped(cx)
    prev = prev_sc[...]
    y3 = scw_ref[nsc:nsc + 1, :] * cx3
    for s in range(1, nsc + 1):
        y3 = y3 + scw_ref[nsc - s:nsc - s + 1, :] * _shift_rows(cx3, prev, s, sub)
    prev_sc[...] = cx[tt - 8:tt, :]
    ysc_ref[...] = (scb_ref[...] * y3.reshape(tt, w)).astype(ysc_ref.dtype)

    @pl.when(t == pl.num_programs(1) - 1)
    def _():
        hout_ref[...] = hcarry[...]
        rgbuf_out_ref[...] = prev_rg[8 - nrg:8, :]
        scbuf_out_ref[...] = prev_sc[8 - nsc:8, :]


def _mixer_prompt(zr, h0, rg_buf, sc_buf, rg_conv_w, rg_conv_b, gate_w, rg_gate_b, rg_lambda,
                  sc_conv_w, layer, batch, seq, tt):
    rows = zr.shape[0]
    w = h0.shape[-1]
    nt = seq // tt
    nrg, nsc = rg_buf.shape[1], sc_buf.shape[1]

    def zcol(cidx):
        return pl.BlockSpec((tt, w), lambda b, t: (b * nt + t, cidx))

    def per_batch(n):
        return pl.BlockSpec((None, n, w), lambda b, t: (b, 0, 0))

    def per_layer(n):
        return pl.BlockSpec((None, n, w), lambda b, t: (layer, 0, 0))

    bw = w // LRU_BLOCKS
    return pl.pallas_call(
        functools.partial(_mixer_kernel, tt=tt),
        grid=(batch, nt),
        in_specs=[zcol(0), zcol(1), zcol(2), zcol(3), zcol(4),
                  per_batch(1), per_batch(nrg), per_batch(nsc),
                  per_layer(nrg + 1), per_layer(1),
                  pl.BlockSpec((None, 2, LRU_BLOCKS, bw, bw), lambda b, t: (layer, 0, 0, 0, 0)),
                  per_layer(2), per_layer(1), per_layer(nsc + 1)],
        out_specs=[pl.BlockSpec((tt, w), lambda b, t: (b * nt + t, 0)),
                   pl.BlockSpec((tt, w), lambda b, t: (b * nt + t, 0)),
                   per_batch(1), per_batch(nrg), per_batch(nsc)],
        out_shape=[jax.ShapeDtypeStruct((rows, w), BF16),
                   jax.ShapeDtypeStruct((rows, w), BF16),
                   jax.ShapeDtypeStruct((batch, 1, w), F32),
                   jax.ShapeDtypeStruct((batch, nrg, w), F32),
                   jax.ShapeDtypeStruct((batch, nsc, w), F32)],
        scratch_shapes=[pltpu.VMEM((8, w), F32), pltpu.VMEM((8, w), F32),
                        pltpu.VMEM((1, w), F32)],
        compiler_params=_cparams("parallel", "arbitrary"),
        name="mixer_prompt",
    )(zr, zr, zr, zr, zr, h0, rg_buf, sc_buf, rg_conv_w, rg_conv_b, gate_w, rg_gate_b,
      rg_lambda, sc_conv_w)


def _mixer_step_kernel(rgx_ref, rgg_ref, scb_ref, scc_ref, scx_ref, h0_ref, rgbuf_ref, scbuf_ref,
                       cw_ref, cb_ref, gw_ref, gb_ref, lam_ref, scw_ref,
                       yrg_ref, ysc_ref, hout_ref, rgbuf_out_ref, scbuf_out_ref, *, nrg, nsc):
    w = h0_ref.shape[1]
    x = rgx_ref[...]
    xc = cb_ref[...] + cw_ref[nrg:nrg + 1, :] * x
    for j in range(nrg):
        xc = xc + cw_ref[j:j + 1, :] * rgbuf_ref[:, j * w:(j + 1) * w]
    a, u = _lru_gates(xc, gw_ref, gb_ref, lam_ref)
    h = a * h0_ref[...] + u
    hout_ref[...] = h
    yrg_ref[...] = (h * _gelu_tanh(rgg_ref[...])).astype(yrg_ref.dtype)
    for j in range(nrg - 1):
        rgbuf_out_ref[:, j * w:(j + 1) * w] = rgbuf_ref[:, (j + 1) * w:(j + 2) * w]
    rgbuf_out_ref[:, (nrg - 1) * w:] = x

    cx = scc_ref[...] * scx_ref[...]
    y = scw_ref[nsc:nsc + 1, :] * cx
    for j in range(nsc):
        y = y + scw_ref[j:j + 1, :] * scbuf_ref[:, j * w:(j + 1) * w]
    ysc_ref[...] = (scb_ref[...] * y).astype(ysc_ref.dtype)
    for j in range(nsc - 1):
        scbuf_out_ref[:, j * w:(j + 1) * w] = scbuf_ref[:, (j + 1) * w:(j + 2) * w]
    scbuf_out_ref[:, (nsc - 1) * w:] = cx


def _mixer_sample(zr, h0, rg_buf, sc_buf, rg_conv_w, rg_conv_b, gate_w, rg_gate_b, rg_lambda,
                  sc_conv_w, layer):
    nb, w = h0.shape
    nrg, nsc = rg_buf.shape[1], sc_buf.shape[1]
    bw = w // LRU_BLOCKS

    def zcol(cidx):
        return pl.BlockSpec((nb, w), lambda i: (0, cidx))

    def full(n):
        return pl.BlockSpec((nb, n * w), lambda i: (0, 0))

    def per_layer(n):
        return pl.BlockSpec((None, n, w), lambda i: (layer, 0, 0))

    outs = pl.pallas_call(
        functools.partial(_mixer_step_kernel, nrg=nrg, nsc=nsc),
        grid=(1,),
        in_specs=[zcol(0), zcol(1), zcol(2), zcol(3), zcol(4),
                  full(1), full(nrg), full(nsc),
                  per_layer(nrg + 1), per_layer(1),
                  pl.BlockSpec((None, 2, LRU_BLOCKS, bw, bw), lambda i: (layer, 0, 0, 0, 0)),
                  per_layer(2), per_layer(1), per_layer(nsc + 1)],
        out_specs=[full(1), full(1), full(1), full(nrg), full(nsc)],
        out_shape=[jax.ShapeDtypeStruct((nb, w), BF16),
                   jax.ShapeDtypeStruct((nb, w), BF16),
                   jax.ShapeDtypeStruct((nb, w), F32),
                   jax.ShapeDtypeStruct((nb, nrg * w), F32),
                   jax.ShapeDtypeStruct((nb, nsc * w), F32)],
        compiler_params=_cparams("arbitrary"),
        name="mixer_sample",
    )(zr, zr, zr, zr, zr, h0, rg_buf.reshape(nb, nrg * w), sc_buf.reshape(nb, nsc * w),
      rg_conv_w, rg_conv_b, gate_w, rg_gate_b, rg_lambda, sc_conv_w)
    y_rg, y_sc, h, rgb, scb = outs
    return y_rg, y_sc, h, rgb.reshape(nb, nrg, w), scb.reshape(nb, nsc, w)


def _merge_kernel(x_ref, o_ref, yrg_ref, ysc_ref, gz0_ref, gz1_ref, gz2_ref, g_ref,
                  wa_ref, wr_ref, ws_ref, wo_ref, out_ref):
    m = _sigmoid(gz0_ref[...]) * _dot(o_ref[...], wa_ref[...])
    m = m + _sigmoid(gz1_ref[...]) * _dot(yrg_ref[...], wr_ref[...])
    m = m + _sigmoid(gz2_ref[...]) * _dot(ysc_ref[...], ws_ref[...])
    y = _dot(m.astype(BF16), wo_ref[...])
    out_ref[...] = x_ref[...] + _rms(y, g_ref[3:4, :])


def _merge(x, o, y_rg, y_sc, zr, norm_g, w_a, w_r, w_s, w_o, layer, tm, gz_col):
    rows, d = x.shape

    def rowblk(cidx=0):
        return pl.BlockSpec((tm, d), lambda i: (i, cidx))

    def wspec(arr):
        return _resident((None,) + arr.shape[1:], lambda i: (layer, 0, 0))

    return pl.pallas_call(
        _merge_kernel,
        grid=(rows // tm,),
        in_specs=[rowblk(), rowblk(), rowblk(), rowblk(),
                  rowblk(gz_col), rowblk(gz_col + 1), rowblk(gz_col + 2),
                  _resident((None, 6, d), lambda i: (layer, 0, 0)),
                  wspec(w_a), wspec(w_r), wspec(w_s), wspec(w_o)],
        out_specs=rowblk(),
        out_shape=jax.ShapeDtypeStruct((rows, d), F32),
        compiler_params=_cparams("parallel"),
        name="merge",
    )(x, o, y_rg, y_sc, zr, zr, zr, norm_g, w_a, w_r, w_s, w_o)


def _pick_tile(n, pref):
    t = min(n, pref)
    while n % t:
        t //= 2
    return t


def kernel(x_prompt, x_sample, cache_k, cache_v, page_table, state_rglru_h, state_rglru_conv, state_sconv, norm_g, w_ffn_up, w_ffn_down, w_in, lambda_qk, subln_g, rg_conv_w, rg_conv_b, rg_gate_w, rg_gate_b, rg_lambda, sc_conv_w, w_branch_attn, w_branch_rg, w_branch_sc, w_out):
    batch, seq, d = x_prompt.shape
    nb, dec_seq, _ = x_sample.shape
    assert dec_seq == 1, "the sample group carries one new token per sequence"
    depth = w_in.shape[0]
    n_pool, page = cache_k.shape[1], cache_k.shape[2]
    head_dim = cache_k.shape[-1]
    hd2 = 2 * head_dim
    qw = N_HEADS * hd2
    kw = N_KV_HEADS * hd2
    w = state_rglru_h.shape[-1]
    assert qw == d and w == d and qw + 2 * kw == 2 * d
    scale = head_dim ** -0.5
    assert math.log2(scale) == round(math.log2(scale)), "q pre-scaling must be exact in bf16"
    n_pages = page_table.shape[1]
    nrg, nsc = state_rglru_conv.shape[2], state_sconv.shape[2]

    w_up_b = w_ffn_up.astype(BF16)
    w_dn_b = w_ffn_down.astype(BF16)
    w_in_b = w_in.astype(BF16)
    w_a_b = w_branch_attn.astype(BF16)
    w_r_b = w_branch_rg.astype(BF16)
    w_s_b = w_branch_sc.astype(BF16)
    w_o_b = w_out.astype(BF16)
    gate_w_b = (0.5 * rg_gate_w).astype(BF16)
    gate_b_half = 0.5 * rg_gate_b

    slopes_np = np.float32(2.0) ** (-8.0 * np.arange(1, N_HEADS + 1, dtype=np.float32) / N_HEADS)
    assert all(math.frexp(float(s))[0] == 0.5 for s in slopes_np), "slopes must be bf16-exact"
    slopes = jnp.asarray(slopes_np, F32)
    cache_kt = jnp.transpose(cache_k, (0, 1, 3, 4, 5, 2)).reshape(depth, n_pool, kw, page)
    cache_v2 = cache_v.reshape(depth, n_pool, page * N_KV_HEADS, hd2)
    rg_conv_b3 = rg_conv_b.reshape(depth, 1, w)
    rg_lambda3 = rg_lambda.reshape(depth, 1, w)
    subln_g3 = subln_g.reshape(depth, 1, hd2)
    eye_kv = jnp.eye(N_KV_HEADS, dtype=BF16)
    eye_m = jnp.eye(2, dtype=BF16)

    tm = _pick_tile(batch * seq, 512)
    tm_wide = _pick_tile(batch * seq, 1024)
    tq = _pick_tile(seq, 256)
    tt = _pick_tile(seq, 512)
    pps = _pick_tile(n_pages, 32)
    gz_col = 5

    xp = x_prompt.reshape(batch * seq, d)
    xs = x_sample.reshape(nb, d)
    zeros_h = jnp.zeros((batch, 1, w), F32)
    zeros_rg = jnp.zeros((batch, nrg, w), F32)
    zeros_sc = jnp.zeros((batch, nsc, w), F32)

    p_states, s_states = [], []
    for l in range(depth):
        lam_init = 0.8 - 0.6 * math.exp(-0.3 * l)
        dense = dict(norm_g=norm_g, layer=l)

        xp = _ffn(xp, norm_g, w_up_b, w_dn_b, l, 0, tm_wide)
        q, k, v, zr = _inproj(xp, norm_g, w_in_b, l, tm_wide, qw, kw, scale)
        o = _attn_prompt(q, k, v, slopes, lambda_qk, subln_g3, l, batch, seq, tq, lam_init)
        y_rg, y_sc, h_p, rgb_p, scb_p = _mixer_prompt(
            zr, zeros_h, zeros_rg, zeros_sc, rg_conv_w, rg_conv_b3, gate_w_b, gate_b_half,
            rg_lambda3, sc_conv_w, l, batch, seq, tt)
        xp = _merge(xp, o, y_rg, y_sc, zr, norm_g, w_a_b, w_r_b, w_s_b, w_o_b, l, tm, gz_col)
        xp = _ffn(xp, norm_g, w_up_b, w_dn_b, l, 1, tm_wide)
        p_states.append((k.reshape(batch, seq, N_KV_HEADS, 2, head_dim),
                         v.reshape(batch, seq, N_KV_HEADS, hd2),
                         h_p.reshape(batch, w), rgb_p, scb_p))

        xs = _ffn(xs, norm_g, w_up_b, w_dn_b, l, 0, nb)
        q, k, v, zr = _inproj(xs, norm_g, w_in_b, l, nb, qw, kw, scale)
        q5 = q.reshape(nb, N_KV_HEADS, HEAD_GROUP, 2, head_dim)
        qt = jnp.einsum('bkgmd,kK,mM->bkgmKMd', q5, eye_kv, eye_m).reshape(nb, 2 * N_HEADS, kw)
        o = _attn_paged(qt, k.reshape(nb, 1, kw), v.reshape(nb, 1, kw), cache_kt, cache_v2,
                        page_table, slopes, lambda_qk, subln_g3, l, pps, lam_init)
        o = o.reshape(nb, qw)
        y_rg, y_sc, h_s, rgb_s, scb_s = _mixer_sample(
            zr, state_rglru_h[l], state_rglru_conv[l], state_sconv[l], rg_conv_w, rg_conv_b3,
            gate_w_b, gate_b_half, rg_lambda3, sc_conv_w, l)
        xs = _merge(xs, o, y_rg, y_sc, zr, norm_g, w_a_b, w_r_b, w_s_b, w_o_b, l, nb, gz_col)
        xs = _ffn(xs, norm_g, w_up_b, w_dn_b, l, 1, nb)
        s_states.append((k.reshape(nb, 1, N_KV_HEADS, 2, head_dim),
                         v.reshape(nb, 1, N_KV_HEADS, hd2), h_s, rgb_s, scb_s))

    k_p, v_p, h_p, rgc_p, sc_p = [jnp.stack(s, axis=0) for s in zip(*p_states)]
    k_s, v_s, h_s, rgc_s, sc_s = [jnp.stack(s, axis=0) for s in zip(*s_states)]
    return (xp.reshape(batch, seq, d), xs.reshape(nb, 1, d), k_p, v_p, h_p, rgc_p, sc_p,
            k_s, v_s, h_s, rgc_s, sc_s)
```

```python
import functools
import math

import numpy as np
import jax
import jax.numpy as jnp
from jax import lax
from jax.experimental import pallas as pl
from jax.experimental.pallas import tpu as pltpu

F32 = jnp.float32
BF16 = jnp.bfloat16

NORM_EPS = 1e-6
LRU_C = 8.0
N_HEADS = 8
N_KV_HEADS = 4
HEAD_GROUP = N_HEADS // N_KV_HEADS
LRU_BLOCKS = 8
N_BRANCHES = 3
MASK_VALUE = -1e30
LOG2_E = math.log2(math.e)
VMEM_LIMIT_BYTES = 56 * 1024 * 1024


def _cparams(*sem):
    return pltpu.CompilerParams(dimension_semantics=sem, vmem_limit_bytes=VMEM_LIMIT_BYTES)


def _rms(x, g):
    return x * lax.rsqrt(jnp.mean(x * x, axis=-1, keepdims=True) + NORM_EPS) * g


def _dot(a, b):
    return jnp.dot(a, b, preferred_element_type=F32)


def _dot_nt(a, b):
    return lax.dot_general(a, b, (((1,), (1,)), ((), ())), preferred_element_type=F32)


def _sigmoid(x):
    return 0.5 * jnp.tanh(0.5 * x) + 0.5


def _gelu_tanh(x):
    c = math.sqrt(2.0 / math.pi)
    return 0.5 * x * (1.0 + jnp.tanh(c * (x + 0.044715 * (x * x * x))))


def _resident(shape, index_map):
    return pl.BlockSpec(shape, index_map, pipeline_mode=pl.Buffered(1))


def _ffn_kernel(x_ref, g_ref, wup_ref, wdn_ref, o_ref, *, d_ff, chunks, g_pre, g_post):
    x = x_ref[...]
    h = _rms(x, g_ref[g_pre:g_pre + 1, :]).astype(BF16)
    acc = None
    for c0, cw in chunks:
        gate = _dot(h, wup_ref[:, c0:c0 + cw])
        up = _dot(h, wup_ref[:, d_ff + c0:d_ff + c0 + cw])
        act = (gate * _sigmoid(gate) * up).astype(BF16)
        part = _dot(act, wdn_ref[c0:c0 + cw, :])
        acc = part if acc is None else acc + part
    o_ref[...] = x + 0.5 * _rms(acc, g_ref[g_post:g_post + 1, :])


def _ffn(x, norm_g, w_up, w_dn, layer, which, tm):
    rows, d = x.shape
    d_ff = w_dn.shape[2]
    chunk = 1024
    chunks = tuple((c0, min(chunk, d_ff - c0)) for c0 in range(0, d_ff, chunk))
    kern = functools.partial(_ffn_kernel, d_ff=d_ff, chunks=chunks,
                             g_pre=0 if which == 0 else 4, g_post=1 if which == 0 else 5)
    return pl.pallas_call(
        kern,
        grid=(rows // tm,),
        in_specs=[
            pl.BlockSpec((tm, d), lambda i: (i, 0)),
            _resident((None, 6, d), lambda i: (layer, 0, 0)),
            _resident((None, None, d, 2 * d_ff), lambda i: (layer, which, 0, 0)),
            _resident((None, None, d_ff, d), lambda i: (layer, which, 0, 0)),
        ],
        out_specs=pl.BlockSpec((tm, d), lambda i: (i, 0)),
        out_shape=jax.ShapeDtypeStruct((rows, d), F32),
        compiler_params=_cparams("parallel"),
        name=f"ffn{which}",
    )(x, norm_g, w_up, w_dn)


def _qkv_kernel(x_ref, g_ref, w_ref, q_ref, k_ref, v_ref, *, qw, kw, scale):
    h = _rms(x_ref[...], g_ref[2:3, :]).astype(BF16)
    z = _dot(h, w_ref[...])
    q_ref[...] = (z[:, :qw] * scale).astype(BF16)
    k_ref[...] = z[:, qw:qw + kw]
    v_ref[...] = z[:, qw + kw:]


def _zrest_kernel(x_ref, g_ref, w_ref, z_ref):
    h = _rms(x_ref[...], g_ref[2:3, :]).astype(BF16)
    z_ref[...] = _dot(h, w_ref[...])


def _inproj(x, norm_g, w_in, layer, tm, qw, kw, scale):
    rows, d = x.shape
    in_w = w_in.shape[2]
    cw = qw + 2 * kw
    n_rest = (in_w - cw) // cw
    assert cw * (n_rest + 1) == in_w
    q, k, v = pl.pallas_call(
        functools.partial(_qkv_kernel, qw=qw, kw=kw, scale=scale),
        grid=(rows // tm,),
        in_specs=[
            pl.BlockSpec((tm, d), lambda i: (i, 0)),
            _resident((None, 6, d), lambda i: (layer, 0, 0)),
            _resident((None, d, cw), lambda i: (layer, 0, 0)),
        ],
        out_specs=[
            pl.BlockSpec((tm, qw), lambda i: (i, 0)),
            pl.BlockSpec((tm, kw), lambda i: (i, 0)),
            pl.BlockSpec((tm, kw), lambda i: (i, 0)),
        ],
        out_shape=[
            jax.ShapeDtypeStruct((rows, qw), BF16),
            jax.ShapeDtypeStruct((rows, kw), F32),
            jax.ShapeDtypeStruct((rows, kw), F32),
        ],
        compiler_params=_cparams("parallel"),
        name="inproj_qkv",
    )(x, norm_g, w_in)
    tr = tm
    zr = pl.pallas_call(
        _zrest_kernel,
        grid=(n_rest, rows // tr),
        in_specs=[
            pl.BlockSpec((tr, d), lambda j, i: (i, 0)),
            _resident((None, 6, d), lambda j, i: (layer, 0, 0)),
            pl.BlockSpec((None, d, cw), lambda j, i: (layer, 0, j + 1)),
        ],
        out_specs=pl.BlockSpec((tr, cw), lambda j, i: (i, j)),
        out_shape=jax.ShapeDtypeStruct((rows, in_w - cw), F32),
        compiler_params=_cparams("parallel", "parallel"),
        name="inproj_rest",
    )(x, norm_g, w_in)
    return q, k, v, zr


def _lambda_full(lqk_ref, lam_init):
    s01 = jnp.sum(lqk_ref[0:1, :] * lqk_ref[1:2, :], axis=-1, keepdims=True)
    s23 = jnp.sum(lqk_ref[2:3, :] * lqk_ref[3:4, :], axis=-1, keepdims=True)
    return jnp.exp(s01) - jnp.exp(s23) + lam_init


def _subln(o, sg, lam_init):
    return _rms(o, sg) * (1.0 - lam_init)


def _loop_unrolled(n, fn, unroll):
    def body(jj, carry):
        for u in range(unroll):
            fn(unroll * jj + u)
        return carry

    lax.fori_loop(0, n // unroll, body, 0)
    if unroll == 4:
        base = (n // 4) * 4

        @pl.when(n % 4 >= 2)
        def _():
            fn(base)
            fn(base + 1)

    @pl.when(n % 2 == 1)
    def _():
        fn(n - 1)


def _attn_kernel(slopes_ref, q_ref, k_ref, v_ref, pos_ref, lqk_ref, sg_ref, o_ref,
                 kaug, vbf, s_scr, mx_scr, ls_scr, acc_scr, *, tq, lam_init):
    kvh = pl.program_id(1)
    seq, hd2 = k_ref.shape
    hd = hd2 // 2
    tk = tq
    nc = tk // 128
    nq = seq // tq
    lane = lax.broadcasted_iota(jnp.int32, (1, hd2), 1)

    k = k_ref[...]
    kaug[0] = jnp.where(lane < hd, k, pos_ref[0]).astype(BF16)
    kaug[1] = jnp.where(lane >= hd, k, pos_ref[1]).astype(BF16)
    vbf[...] = v_ref[...].astype(BF16)

    row = lax.broadcasted_iota(jnp.int32, (2 * tq, 1), 0)
    slope = jnp.where(row < tq, slopes_ref[2 * kvh], slopes_ref[2 * kvh + 1])
    diagonal_visible = (lax.broadcasted_iota(jnp.int32, (1, tk), 1)
                        <= jnp.where(row < tq, row, row - tq))
    lam = _lambda_full(lqk_ref, lam_init)

    def q_tile(i):
        q = q_ref[pl.ds(pl.multiple_of(i * tq, tq), tq), :].astype(F32)
        qs = jnp.concatenate([q[:, :hd2], q[:, hd2:]], axis=0)
        return (jnp.where(lane < hd, qs, jnp.where(lane < hd + 2, slope, 0.0)).astype(BF16),
                jnp.where(lane >= hd, qs, jnp.where(lane < 2, slope, 0.0)).astype(BF16))

    def scores(qm, buf, j, on_diagonal):
        k0 = pl.multiple_of(j * tk, tk)
        for mm in range(2):
            s = _dot_nt(qm[mm], kaug[mm, pl.ds(k0, tk), :]) * LOG2_E
            if on_diagonal:
                s = jnp.where(diagonal_visible, s, MASK_VALUE)
            s_scr[buf, mm, j] = s
            mx = mx_scr[buf, mm]
            for c in range(nc):
                mx = jnp.maximum(mx, s[:, c * 128:(c + 1) * 128])
            mx_scr[buf, mm] = mx

    def weigh(buf, j):
        k0 = pl.multiple_of(j * tk, tk)
        vb = vbf[pl.ds(k0, tk), :]
        for mm in range(2):
            s = s_scr[buf, mm, j]
            mb = mx_scr[buf, mm]
            ps = [jnp.exp2(s[:, c * 128:(c + 1) * 128] - mb) for c in range(nc)]
            ls = ls_scr[mm]
            for c in range(nc):
                ls = ls + ps[c]
            ls_scr[mm] = ls
            p = jnp.concatenate(ps, axis=-1).astype(BF16)
            acc_scr[mm] = acc_scr[mm] + _dot(p, vb)

    def open_scores(buf):
        mx_scr[buf] = jnp.full(mx_scr.shape[1:], MASK_VALUE, F32)

    def close_scores(buf):
        for mm in range(2):
            mx_scr[buf, mm] = jnp.broadcast_to(
                jnp.max(mx_scr[buf, mm], axis=-1, keepdims=True), mx_scr.shape[2:])
        ls_scr[...] = jnp.zeros(ls_scr.shape, F32)
        acc_scr[...] = jnp.zeros(acc_scr.shape, F32)

    def emit(i):
        l0 = jnp.sum(ls_scr[0], axis=-1, keepdims=True)
        l1 = jnp.sum(ls_scr[1], axis=-1, keepdims=True)
        o = acc_scr[0] / l0 - lam * (acc_scr[1] / l1)
        o = _subln(o, sg_ref[...], lam_init).astype(o_ref.dtype)
        r0 = pl.multiple_of(i * tq, tq)
        o_ref[pl.ds(r0, tq), :hd2] = o[:tq]
        o_ref[pl.ds(r0, tq), hd2:] = o[tq:]

    open_scores(0)
    scores(q_tile(0), 0, 0, True)

    def q_step(i, carry):
        cur = i % 2
        nxt = 1 - cur
        close_scores(cur)
        q_next = q_tile(i + 1)
        open_scores(nxt)

        _loop_unrolled(i, lambda j: weigh(cur, j), 4)
        _loop_unrolled(i + 1, lambda j: scores(q_next, nxt, j, False), 4)
        weigh(cur, i)
        scores(q_next, nxt, i + 1, True)
        emit(i)
        return carry

    lax.fori_loop(0, nq - 1, q_step, 0)
    last = (nq - 1) % 2
    close_scores(last)
    _loop_unrolled(nq, lambda j: weigh(last, j), 4)
    emit(nq - 1)


def _attn_prompt(q, k, v, slopes, lambda_qk, subln_g, layer, batch, seq, tq, lam_init):
    rows, qw = q.shape
    hd2 = k.shape[1] // N_KV_HEADS
    nq = seq // tq
    assert tq % 128 == 0 and seq % tq == 0
    kern = functools.partial(_attn_kernel, tq=tq, lam_init=lam_init)
    hd = hd2 // 2
    pos = np.arange(seq)
    pos_lanes = np.zeros((2, seq, hd2), np.float32)
    pos_lanes[0, :, hd], pos_lanes[0, :, hd + 1] = pos - pos % 8, pos % 8
    pos_lanes[1, :, 0], pos_lanes[1, :, 1] = pos - pos % 8, pos % 8
    assert seq <= 8 * 256, "position parts must stay exact in bf16"
    return pl.pallas_call(
        kern,
        grid_spec=pltpu.PrefetchScalarGridSpec(
            num_scalar_prefetch=1,
            grid=(batch, N_KV_HEADS),
            in_specs=[
                pl.BlockSpec((seq, HEAD_GROUP * hd2), lambda b, h, s: (b, h)),
                pl.BlockSpec((seq, hd2), lambda b, h, s: (b, h)),
                pl.BlockSpec((seq, hd2), lambda b, h, s: (b, h)),
                _resident((2, seq, hd2), lambda b, h, s: (0, 0, 0)),
                pl.BlockSpec((None, 4, hd2 // 2), lambda b, h, s: (layer, 0, 0)),
                pl.BlockSpec((None, 1, hd2), lambda b, h, s: (layer, 0, 0)),
            ],
            out_specs=pl.BlockSpec((seq, HEAD_GROUP * hd2), lambda b, h, s: (b, h)),
            scratch_shapes=[
                pltpu.VMEM((2, seq, hd2), BF16),
                pltpu.VMEM((seq, hd2), BF16),
                pltpu.VMEM((2, 2, nq, 2 * tq, tq), F32),
                pltpu.VMEM((2, 2, 2 * tq, 128), F32),
                pltpu.VMEM((2, 2 * tq, 128), F32),
                pltpu.VMEM((2, 2 * tq, hd2), F32),
            ],
        ),
        out_shape=jax.ShapeDtypeStruct((rows, qw), BF16),
        compiler_params=_cparams("parallel", "parallel"),
        name="attn_prompt",
    )(slopes, q, k, v, jnp.asarray(pos_lanes), lambda_qk, subln_g)


def _paged_kernel(pt_ref, slopes_ref, qt_ref, ks_ref, vs_ref, lqk_ref, sg_ref, *refs,
                  pages_per_step, page, past, lam_init):
    del pt_ref
    pp = pages_per_step
    kt_refs = refs[:pp]
    v_refs = refs[pp:2 * pp]
    o_ref = refs[2 * pp]
    m_ref, l_ref, acc_ref = refs[2 * pp + 1:]
    c = pl.program_id(1)
    nrow = qt_ref.shape[0]
    hd2 = sg_ref.shape[1]
    qt = qt_ref[...]
    row = lax.broadcasted_iota(jnp.int32, (nrow, 1), 0)
    row_kv = row // (2 * HEAD_GROUP)
    head = row // 2
    slope = jnp.zeros((nrow, 1), F32)
    for h in range(N_HEADS):
        slope = jnp.where(head == h, slopes_ref[h], slope)

    @pl.when(c == 0)
    def _():
        ks = ks_ref[...].astype(BF16).astype(F32)
        m_ref[...] = jnp.sum(qt.astype(F32) * ks, axis=-1, keepdims=True)
        l_ref[...] = jnp.ones(l_ref.shape, F32)
        vs = vs_ref[...].astype(BF16).astype(F32)
        a = jnp.zeros(acc_ref.shape, F32)
        for kv in range(N_KV_HEADS):
            a = jnp.where(row_kv == kv, vs[:, kv * hd2:(kv + 1) * hd2], a)
        acc_ref[...] = a

    s_parts = []
    for r in range(pp):
        kpos = (c * pp + r) * page + lax.broadcasted_iota(jnp.int32, (1, page), 1)
        dist = (past - kpos).astype(F32)
        s_parts.append(_dot(qt, kt_refs[r][...].astype(BF16)) - slope * dist)
    s = jnp.concatenate(s_parts, axis=-1)
    m_old = m_ref[...]
    m_new = jnp.maximum(m_old, jnp.max(s, axis=-1, keepdims=True))
    p = jnp.exp(s - m_new)
    corr = jnp.exp(m_old - m_new)
    l_ref[...] = corr * l_ref[...] + jnp.sum(p, axis=-1, keepdims=True)
    pv = None
    for kv in range(N_KV_HEADS):
        pk = jnp.where(row_kv == kv, p, 0.0).astype(BF16)
        for r in range(pp):
            v_kv = v_refs[r][pl.ds(kv, page, stride=N_KV_HEADS), :].astype(BF16)
            part = _dot(pk[:, r * page:(r + 1) * page], v_kv)
            pv = part if pv is None else pv + part
    acc_ref[...] = corr * acc_ref[...] + pv
    m_ref[...] = m_new

    @pl.when(c == pl.num_programs(1) - 1)
    def _():
        lam = _lambda_full(lqk_ref, lam_init)
        acc_ref[...] = acc_ref[...] / l_ref[...]
        a0 = acc_ref[pl.ds(0, N_HEADS, stride=2), :]
        a1 = acc_ref[pl.ds(1, N_HEADS, stride=2), :]
        o_ref[...] = _subln(a0 - lam * a1, sg_ref[...], lam_init).astype(o_ref.dtype)


def _attn_paged(qt, k_new, v_new, cache_kt, cache_v2, page_table, slopes, lambda_qk, subln_g,
                layer, pages_per_step, lam_init):
    nb, nrow, kvw = qt.shape
    page = cache_kt.shape[3]
    n_pages = page_table.shape[1]
    pp = pages_per_step
    hd2 = kvw // N_KV_HEADS
    assert cache_kt.shape[2] == kvw and cache_v2.shape[2:] == (page * N_KV_HEADS, hd2)
    kern = functools.partial(_paged_kernel, pages_per_step=pp, page=page,
                             past=n_pages * page, lam_init=lam_init)

    def page_spec(arr, r):
        return pl.BlockSpec((None, None) + arr.shape[2:],
                            lambda b, c, pt, s: (layer, pt[b, c * pp + r], 0, 0))

    return pl.pallas_call(
        kern,
        grid_spec=pltpu.PrefetchScalarGridSpec(
            num_scalar_prefetch=2,
            grid=(nb, n_pages // pp),
            in_specs=[
                pl.BlockSpec((None, nrow, kvw), lambda b, c, pt, s: (b, 0, 0)),
                pl.BlockSpec((None, 1, kvw), lambda b, c, pt, s: (b, 0, 0)),
                pl.BlockSpec((None, 1, kvw), lambda b, c, pt, s: (b, 0, 0)),
                pl.BlockSpec((None, 4, hd2 // 2), lambda b, c, pt, s: (layer, 0, 0)),
                pl.BlockSpec((None, 1, hd2), lambda b, c, pt, s: (layer, 0, 0)),
            ] + [page_spec(cache_kt, r) for r in range(pp)]
              + [page_spec(cache_v2, r) for r in range(pp)],
            out_specs=pl.BlockSpec((None, N_HEADS, hd2), lambda b, c, pt, s: (b, 0, 0)),
            scratch_shapes=[
                pltpu.VMEM((nrow, 1), F32),
                pltpu.VMEM((nrow, 1), F32),
                pltpu.VMEM((nrow, hd2), F32),
            ],
        ),
        out_shape=jax.ShapeDtypeStruct((nb, N_HEADS, hd2), BF16),
        compiler_params=_cparams("parallel", "arbitrary"),
        name="attn_paged",
    )(page_table, slopes, qt, k_new, v_new, lambda_qk, subln_g,
      *([cache_kt] * pp), *([cache_v2] * pp))


def _lru_gates(xc, gw_ref, gb_ref, lam_ref):
    w = xc.shape[1]
    bw = w // LRU_BLOCKS
    g0, g1 = [], []
    for n in range(LRU_BLOCKS):
        xb = xc[:, n * bw:(n + 1) * bw].astype(BF16)
        g0.append(_dot(xb, gw_ref[0, n]))
        g1.append(_dot(xb, gw_ref[1, n]))
    t_r = jnp.tanh(jnp.concatenate(g0, axis=-1) + gb_ref[0:1, :])
    i = 0.5 * jnp.tanh(jnp.concatenate(g1, axis=-1) + gb_ref[1:2, :]) + 0.5
    neg_lam = -lam_ref[...]
    softplus = jnp.maximum(neg_lam, 0.0) + jnp.log1p(jnp.exp(-jnp.abs(neg_lam)))
    half_e = (-0.5 * LRU_C * LOG2_E) * softplus
    a = jnp.exp2(t_r * half_e + half_e)
    d = 1.0 - a * a
    root = jnp.where(d > 0.0, d * lax.rsqrt(d), 0.0)
    u = root * (i * xc)
    return a, u


def _grouped(x):
    return x.reshape(x.shape[0] // 8, 8, x.shape[1])


def _shift_rows(x3, prev8, s, sub):
    rolled = pltpu.roll(jnp.concatenate([prev8[None], x3], axis=0), s, 1)
    return jnp.where(sub >= s, rolled[1:], rolled[:-1])


def _scan_groups(a3, u3, h_in):
    sub = lax.broadcasted_iota(jnp.int32, (1, 8, 1), 1)
    for s in (1, 2, 4):
        keep = sub >= s
        a_sh = jnp.where(keep, pltpu.roll(a3, s, 1), 1.0)
        u_sh = jnp.where(keep, pltpu.roll(u3, s, 1), 0.0)
        u3 = a3 * u_sh + u3
        a3 = a3 * a_sh
    out = []
    h_prev = h_in
    for g in range(a3.shape[0]):
        hg = u3[g] + a3[g] * h_prev
        out.append(hg)
        h_prev = hg[7:8, :]
    return jnp.concatenate(out, axis=0)


def _mixer_kernel(rgx_ref, rgg_ref, scb_ref, scc_ref, scx_ref, h0_ref, rgbuf_ref, scbuf_ref,
                  cw_ref, cb_ref, gw_ref, gb_ref, lam_ref, scw_ref,
                  yrg_ref, ysc_ref, hout_ref, rgbuf_out_ref, scbuf_out_ref,
                  prev_rg, prev_sc, hcarry, *, tt):
    t = pl.program_id(1)
    nrg = rgbuf_ref.shape[0]
    nsc = scbuf_ref.shape[0]
    w = rgx_ref.shape[1]
    sub = lax.broadcasted_iota(jnp.int32, (1, 8, 1), 1)

    @pl.when(t == 0)
    def _():
        prev_rg[...] = jnp.zeros(prev_rg.shape, F32)
        prev_sc[...] = jnp.zeros(prev_sc.shape, F32)
        prev_rg[8 - nrg:8, :] = rgbuf_ref[...]
        prev_sc[8 - nsc:8, :] = scbuf_ref[...]
        hcarry[...] = h0_ref[...]

    x = rgx_ref[...]
    x3 = _grouped(x)
    prev = prev_rg[...]
    xc3 = cb_ref[...] + cw_ref[nrg:nrg + 1, :] * x3
    for s in range(1, nrg + 1):
        xc3 = xc3 + cw_ref[nrg - s:nrg - s + 1, :] * _shift_rows(x3, prev, s, sub)
    prev_rg[...] = x[tt - 8:tt, :]
    a, u = _lru_gates(xc3.reshape(tt, w), gw_ref, gb_ref, lam_ref)
    h = _scan_groups(_grouped(a), _grouped(u), hcarry[...])
    hcarry[...] = h[tt - 1:tt, :]
    yrg_ref[...] = (h * _gelu_tanh(rgg_ref[...])).astype(yrg_ref.dtype)

    cx = scc_ref[...] * scx_ref[...]
    cx3 = _grouped(cx)
    prev = prev_sc[...]
    y3 = scw_ref[nsc:nsc + 1, :] * cx3
    for s in range(1, nsc + 1):
        y3 = y3 + scw_ref[nsc - s:nsc - s + 1, :] * _shift_rows(cx3, prev, s, sub)
    prev_sc[...] = cx[tt - 8:tt, :]
    ysc_ref[...] = (scb_ref[...] * y3.reshape(tt, w)).astype(ysc_ref.dtype)

    @pl.when(t == pl.num_programs(1) - 1)
    def _():
        hout_ref[...] = hcarry[...]
        rgbuf_out_ref[...] = prev_rg[8 - nrg:8, :]
        scbuf_out_ref[...] = prev_sc[8 - nsc:8, :]


def _mixer_prompt(zr, h0, rg_buf, sc_buf, rg_conv_w, rg_conv_b, gate_w, rg_gate_b, rg_lambda,
                  sc_conv_w, layer, batch, seq, tt):
    rows = zr.shape[0]
    w = h0.shape[-1]
    nt = seq // tt
    nrg, nsc = rg_buf.shape[1], sc_buf.shape[1]

    def zcol(cidx):
        return pl.BlockSpec((tt, w), lambda b, t: (b * nt + t, cidx))

    def per_batch(n):
        return pl.BlockSpec((None, n, w), lambda b, t: (b, 0, 0))

    def per_layer(n):
        return pl.BlockSpec((None, n, w), lambda b, t: (layer, 0, 0))

    bw = w // LRU_BLOCKS
    return pl.pallas_call(
        functools.partial(_mixer_kernel, tt=tt),
        grid=(batch, nt),
        in_specs=[zcol(0), zcol(1), zcol(2), zcol(3), zcol(4),
                  per_batch(1), per_batch(nrg), per_batch(nsc),
                  per_layer(nrg + 1), per_layer(1),
                  pl.BlockSpec((None, 2, LRU_BLOCKS, bw, bw), lambda b, t: (layer, 0, 0, 0, 0)),
                  per_layer(2), per_layer(1), per_layer(nsc + 1)],
        out_specs=[pl.BlockSpec((tt, w), lambda b, t: (b * nt + t, 0)),
                   pl.BlockSpec((tt, w), lambda b, t: (b * nt + t, 0)),
                   per_batch(1), per_batch(nrg), per_batch(nsc)],
        out_shape=[jax.ShapeDtypeStruct((rows, w), BF16),
                   jax.ShapeDtypeStruct((rows, w), BF16),
                   jax.ShapeDtypeStruct((batch, 1, w), F32),
                   jax.ShapeDtypeStruct((batch, nrg, w), F32),
                   jax.ShapeDtypeStruct((batch, nsc, w), F32)],
        scratch_shapes=[pltpu.VMEM((8, w), F32), pltpu.VMEM((8, w), F32),
                        pltpu.VMEM((1, w), F32)],
        compiler_params=_cparams("parallel", "arbitrary"),
        name="mixer_prompt",
    )(zr, zr, zr, zr, zr, h0, rg_buf, sc_buf, rg_conv_w, rg_conv_b, gate_w, rg_gate_b,
      rg_lambda, sc_conv_w)


def _mixer_step_kernel(rgx_ref, rgg_ref, scb_ref, scc_ref, scx_ref, h0_ref, rgbuf_ref, scbuf_ref,
                       cw_ref, cb_ref, gw_ref, gb_ref, lam_ref, scw_ref,
                       yrg_ref, ysc_ref, hout_ref, rgbuf_out_ref, scbuf_out_ref, *, nrg, nsc):
    w = h0_ref.shape[1]
    x = rgx_ref[...]
    xc = cb_ref[...] + cw_ref[nrg:nrg + 1, :] * x
    for j in range(nrg):
        xc = xc + cw_ref[j:j + 1, :] * rgbuf_ref[:, j * w:(j + 1) * w]
    a, u = _lru_gates(xc, gw_ref, gb_ref, lam_ref)
    h = a * h0_ref[...] + u
    hout_ref[...] = h
    yrg_ref[...] = (h * _gelu_tanh(rgg_ref[...])).astype(yrg_ref.dtype)
    for j in range(nrg - 1):
        rgbuf_out_ref[:, j * w:(j + 1) * w] = rgbuf_ref[:, (j + 1) * w:(j + 2) * w]
    rgbuf_out_ref[:, (nrg - 1) * w:] = x

    cx = scc_ref[...] * scx_ref[...]
    y = scw_ref[nsc:nsc + 1, :] * cx
    for j in range(nsc):
        y = y + scw_ref[j:j + 1, :] * scbuf_ref[:, j * w:(j + 1) * w]
    ysc_ref[...] = (scb_ref[...] * y).astype(ysc_ref.dtype)
    for j in range(nsc - 1):
        scbuf_out_ref[:, j * w:(j + 1) * w] = scbuf_ref[:, (j + 1) * w:(j + 2) * w]
    scbuf_out_ref[:, (nsc - 1) * w:] = cx


def _mixer_sample(zr, h0, rg_buf, sc_buf, rg_conv_w, rg_conv_b, gate_w, rg_gate_b, rg_lambda,
                  sc_conv_w, layer):
    nb, w = h0.shape
    nrg, nsc = rg_buf.shape[1], sc_buf.shape[1]
    bw = w // LRU_BLOCKS

    def zcol(cidx):
        return pl.BlockSpec((nb, w), lambda i: (0, cidx))

    def full(n):
        return pl.BlockSpec((nb, n * w), lambda i: (0, 0))

    def per_layer(n):
        return pl.BlockSpec((None, n, w), lambda i: (layer, 0, 0))

    outs = pl.pallas_call(
        functools.partial(_mixer_step_kernel, nrg=nrg, nsc=nsc),
        grid=(1,),
        in_specs=[zcol(0), zcol(1), zcol(2), zcol(3), zcol(4),
                  full(1), full(nrg), full(nsc),
                  per_layer(nrg + 1), per_layer(1),
                  pl.BlockSpec((None, 2, LRU_BLOCKS, bw, bw), lambda i: (layer, 0, 0, 0, 0)),
                  per_layer(2), per_layer(1), per_layer(nsc + 1)],
        out_specs=[full(1), full(1), full(1), full(nrg), full(nsc)],
        out_shape=[jax.ShapeDtypeStruct((nb, w), BF16),
                   jax.ShapeDtypeStruct((nb, w), BF16),
                   jax.ShapeDtypeStruct((nb, w), F32),
                   jax.ShapeDtypeStruct((nb, nrg * w), F32),
                   jax.ShapeDtypeStruct((nb, nsc * w), F32)],
        compiler_params=_cparams("arbitrary"),
        name="mixer_sample",
    )(zr, zr, zr, zr, zr, h0, rg_buf.reshape(nb, nrg * w), sc_buf.reshape(nb, nsc * w),
      rg_conv_w, rg_conv_b, gate_w, rg_gate_b, rg_lambda, sc_conv_w)
    y_rg, y_sc, h, rgb, scb = outs
    return y_rg, y_sc, h, rgb.reshape(nb, nrg, w), scb.reshape(nb, nsc, w)


def _merge_kernel(x_ref, o_ref, yrg_ref, ysc_ref, gz0_ref, gz1_ref, gz2_ref, g_ref,
                  wa_ref, wr_ref, ws_ref, wo_ref, out_ref):
    m = _sigmoid(gz0_ref[...]) * _dot(o_ref[...], wa_ref[...])
    m = m + _sigmoid(gz1_ref[...]) * _dot(yrg_ref[...], wr_ref[...])
    m = m + _sigmoid(gz2_ref[...]) * _dot(ysc_ref[...], ws_ref[...])
    y = _dot(m.astype(BF16), wo_ref[...])
    out_ref[...] = x_ref[...] + _rms(y, g_ref[3:4, :])


def _merge(x, o, y_rg, y_sc, zr, norm_g, w_a, w_r, w_s, w_o, layer, tm, gz_col):
    rows, d = x.shape

    def rowblk(cidx=0):
        return pl.BlockSpec((tm, d), lambda i: (i, cidx))

    def wspec(arr):
        return _resident((None,) + arr.shape[1:], lambda i: (layer, 0, 0))

    return pl.pallas_call(
        _merge_kernel,
        grid=(rows // tm,),
        in_specs=[rowblk(), rowblk(), rowblk(), rowblk(),
                  rowblk(gz_col), rowblk(gz_col + 1), rowblk(gz_col + 2),
                  _resident((None, 6, d), lambda i: (layer, 0, 0)),
                  wspec(w_a), wspec(w_r), wspec(w_s), wspec(w_o)],
        out_specs=rowblk(),
        out_shape=jax.ShapeDtypeStruct((rows, d), F32),
        compiler_params=_cparams("parallel"),
        name="merge",
    )(x, o, y_rg, y_sc, zr, zr, zr, norm_g, w_a, w_r, w_s, w_o)


def _pick_tile(n, pref):
    t = min(n, pref)
    while n % t:
        t //= 2
    return t


def kernel(x_prompt, x_sample, cache_k, cache_v, page_table, state_rglru_h, state_rglru_conv, state_sconv, norm_g, w_ffn_up, w_ffn_down, w_in, lambda_qk, subln_g, rg_conv_w, rg_conv_b, rg_gate_w, rg_gate_b, rg_lambda, sc_conv_w, w_branch_attn, w_branch_rg, w_branch_sc, w_out):
    batch, seq, d = x_prompt.shape
    nb, dec_seq, _ = x_sample.shape
    assert dec_seq == 1, "the sample group carries one new token per sequence"
    depth = w_in.shape[0]
    n_pool, page = cache_k.shape[1], cache_k.shape[2]
    head_dim = cache_k.shape[-1]
    hd2 = 2 * head_dim
    qw = N_HEADS * hd2
    kw = N_KV_HEADS * hd2
    w = state_rglru_h.shape[-1]
    assert qw == d and w == d and qw + 2 * kw == 2 * d
    scale = head_dim ** -0.5
    assert math.log2(scale) == round(math.log2(scale)), "q pre-scaling must be exact in bf16"
    n_pages = page_table.shape[1]
    nrg, nsc = state_rglru_conv.shape[2], state_sconv.shape[2]

    w_up_b = w_ffn_up.astype(BF16)
    w_dn_b = w_ffn_down.astype(BF16)
    w_in_b = w_in.astype(BF16)
    w_a_b = w_branch_attn.astype(BF16)
    w_r_b = w_branch_rg.astype(BF16)
    w_s_b = w_branch_sc.astype(BF16)
    w_o_b = w_out.astype(BF16)
    gate_w_b = (0.5 * rg_gate_w).astype(BF16)
    gate_b_half = 0.5 * rg_gate_b

    slopes_np = np.float32(2.0) ** (-8.0 * np.arange(1, N_HEADS + 1, dtype=np.float32) / N_HEADS)
    assert all(math.frexp(float(s))[0] == 0.5 for s in slopes_np), "slopes must be bf16-exact"
    slopes = jnp.asarray(slopes_np, F32)
    cache_kt = jnp.transpose(cache_k, (0, 1, 3, 4, 5, 2)).reshape(depth, n_pool, kw, page)
    cache_v2 = cache_v.reshape(depth, n_pool, page * N_KV_HEADS, hd2)
    rg_conv_b3 = rg_conv_b.reshape(depth, 1, w)
    rg_lambda3 = rg_lambda.reshape(depth, 1, w)
    subln_g3 = subln_g.reshape(depth, 1, hd2)
    eye_kv = jnp.eye(N_KV_HEADS, dtype=BF16)
    eye_m = jnp.eye(2, dtype=BF16)

    tm = _pick_tile(batch * seq, 512)
    tm_wide = _pick_tile(batch * seq, 1024)
    tq = _pick_tile(seq, 256)
    tt = _pick_tile(seq, 512)
    pps = _pick_tile(n_pages, 32)
    gz_col = 5

    xp = x_prompt.reshape(batch * seq, d)
    xs = x_sample.reshape(nb, d)
    zeros_h = jnp.zeros((batch, 1, w), F32)
    zeros_rg = jnp.zeros((batch, nrg, w), F32)
    zeros_sc = jnp.zeros((batch, nsc, w), F32)

    p_states, s_states = [], []
    for l in range(depth):
        lam_init = 0.8 - 0.6 * math.exp(-0.3 * l)
        dense = dict(norm_g=norm_g, layer=l)

        xp = _ffn(xp, norm_g, w_up_b, w_dn_b, l, 0, tm_wide)
        q, k, v, zr = _inproj(xp, norm_g, w_in_b, l, tm_wide, qw, kw, scale)
        o = _attn_prompt(q, k, v, slopes, lambda_qk, subln_g3, l, batch, seq, tq, lam_init)
        y_rg, y_sc, h_p, rgb_p, scb_p = _mixer_prompt(
            zr, zeros_h, zeros_rg, zeros_sc, rg_conv_w, rg_conv_b3, gate_w_b, gate_b_half,
            rg_lambda3, sc_conv_w, l, batch, seq, tt)
        xp = _merge(xp, o, y_rg, y_sc, zr, norm_g, w_a_b, w_r_b, w_s_b, w_o_b, l, tm, gz_col)
        xp = _ffn(xp, norm_g, w_up_b, w_dn_b, l, 1, tm_wide)
        p_states.append((k.reshape(batch, seq, N_KV_HEADS, 2, head_dim),
                         v.reshape(batch, seq, N_KV_HEADS, hd2),
                         h_p.reshape(batch, w), rgb_p, scb_p))

        xs = _ffn(xs, norm_g, w_up_b, w_dn_b, l, 0, nb)
        q, k, v, zr = _inproj(xs, norm_g, w_in_b, l, nb, qw, kw, scale)
        q5 = q.reshape(nb, N_KV_HEADS, HEAD_GROUP, 2, head_dim)
        qt = jnp.einsum('bkgmd,kK,mM->bkgmKMd', q5, eye_kv, eye_m).reshape(nb, 2 * N_HEADS, kw)
        o = _attn_paged(qt, k.reshape(nb, 1, kw), v.reshape(nb, 1, kw), cache_kt, cache_v2,
                        page_table, slopes, lambda_qk, subln_g3, l, pps, lam_init)
        o = o.reshape(nb, qw)
        y_rg, y_sc, h_s, rgb_s, scb_s = _mixer_sample(
            zr, state_rglru_h[l], state_rglru_conv[l], state_sconv[l], rg_conv_w, rg_conv_b3,
            gate_w_b, gate_b_half, rg_lambda3, sc_conv_w, l)
        xs = _merge(xs, o, y_rg, y_sc, zr, norm_g, w_a_b, w_r_b, w_s_b, w_o_b, l, nb, gz_col)
        xs = _ffn(xs, norm_g, w_up_b, w_dn_b, l, 1, nb)
        s_states.append((k.reshape(nb, 1, N_KV_HEADS, 2, head_dim),
                         v.reshape(nb, 1, N_KV_HEADS, hd2), h_s, rgb_s, scb_s))

    k_p, v_p, h_p, rgc_p, sc_p = [jnp.stack(s, axis=0) for s in zip(*p_states)]
    k_s, v_s, h_s, rgc_s, sc_s = [jnp.stack(s, axis=0) for s in zip(*s_states)]
    return (xp.reshape(batch, seq, d), xs.reshape(nb, 1, d), k_p, v_p, h_p, rgc_p, sc_p,
            k_s, v_s, h_s, rgc_s, sc_s)
```
